```python
import math
import jax
import jax.numpy as jnp
from jax import lax
import numpy as np

D_MODEL = 1024
BATCH = 4
SEQ = 4096
DEPTH = 4
DEC_BATCH = 128
DEC_SEQ = 1
PAST_LEN = 8192
PAGE_SIZE = 128

N_MIXERS = 4
Q_BLOCK = 128
EPS = 1e-6
REL_BUCKETS = 32
REL_MAX_DIST = 2048
A_WINDOWS = (128, 512, 2048)
A_DILATIONS = (1, 4, 16)
A_GROUPS = 3
A_HEADS = 8
A_HEAD_DIM = 64
A_KEYS = tuple(w // d + 1 for w, d in zip(A_WINDOWS, A_DILATIONS))
A_SCALE = A_HEAD_DIM ** -0.5
REL_COLS = A_GROUPS * A_HEADS
B_CONV_WIDTH = 31
C_HEADS = 16
C_Q_LORA = 384
C_KV_LORA = 256
C_NOPE = 64
C_ROPE = 32
C_V = 64
C_SCALE = (C_NOPE + C_ROPE) ** -0.5
ROPE_THETA = 10000.0
D_HEADS = 16
D_KV_HEADS = 4
D_HEAD_DIM = 64
D_IDX_HEADS = 8
D_IDX_DIM = 64
D_TOPK_MAX = 256
D_SCALE = D_HEAD_DIM ** -0.5
D_SIZES = (D_HEADS * D_HEAD_DIM, 2 * D_KV_HEADS * D_HEAD_DIM, D_IDX_HEADS * D_IDX_DIM, D_IDX_DIM, D_IDX_HEADS)
D_OFFS = tuple(int(v) for v in np.cumsum((0,) + D_SIZES))
X_HEADS = 4
X_HEAD_DIM = 128
X_SCALE = X_HEAD_DIM ** -0.5
MEM_LEN = 256
D_FF = ((8 * D_MODEL + 3 * 256 - 1) // (3 * 256)) * 256

kernel_name = 'hybrid_dilated_conformer_mla_dsa_decoder_step'


def rmsnorm(x, g):
    xf = x.astype(jnp.float32)
    y = xf * lax.rsqrt(jnp.mean(xf * xf, axis=-1, keepdims=True) + EPS)
    return (y * g.astype(jnp.float32)).astype(x.dtype)


def layernorm(x, g, b):
    xf = x.astype(jnp.float32)
    mu = jnp.mean(xf, axis=-1, keepdims=True)
    xc = xf - mu
    y = xc * lax.rsqrt(jnp.mean(xc * xc, axis=-1, keepdims=True) + EPS)
    return (y * g.astype(jnp.float32) + b.astype(jnp.float32)).astype(x.dtype)


def rel_bucket(dist):
    n = jnp.maximum(dist, 0)
    max_exact = REL_BUCKETS // 2
    nf = jnp.maximum(n, 1).astype(jnp.float32)
    large = max_exact + (jnp.log(nf / max_exact) / math.log(REL_MAX_DIST / max_exact)
                         * (REL_BUCKETS - max_exact)).astype(jnp.int32)
    large = jnp.minimum(large, REL_BUCKETS - 1)
    return jnp.where(n < max_exact, n, large)


def map_query_blocks(fn, n_q):
    starts = jnp.arange(0, n_q, Q_BLOCK, dtype=jnp.int32)
    out = jnp.swapaxes(lax.map(fn, starts), 0, 1)
    return out.reshape((out.shape[0], n_q) + out.shape[3:])


def a_group_bias(rel_bias, g):
    dist = A_DILATIONS[g] * jnp.arange(A_KEYS[g], dtype=jnp.int32)
    return rel_bias[rel_bucket(dist)][:, g * A_HEADS:(g + 1) * A_HEADS].astype(jnp.float32)


def dilated_attend(q, kv, qidx, g, bias):
    kidx = qidx[:, None] - A_DILATIONS[g] * jnp.arange(A_KEYS[g], dtype=jnp.int32)[None, :]
    valid = kidx >= 0
    kvg = jnp.take(kv, jnp.maximum(kidx, 0), axis=1)
    s = jnp.einsum('bqhd,bqkhd->bqhk', q, kvg[:, :, :, 0]).astype(jnp.float32) * A_SCALE + bias.T
    s = jnp.where(valid[None, :, None, :], s, -jnp.inf)
    lse = jax.nn.logsumexp(s, axis=-1)
    p = jnp.exp(s - lse[..., None])
    o = jnp.einsum('bqhk,bqkhd->bqhd', p, kvg[:, :, :, 1].astype(jnp.float32))
    return o, lse


def dilated_mixture(q, kv_groups, qidx_groups, rel_bias):
    outs, lses = [], []
    for g in range(A_GROUPS):
        o, lse = dilated_attend(q[:, :, g], kv_groups[g], qidx_groups[g], g, a_group_bias(rel_bias, g))
        outs.append(o)
        lses.append(lse)
    alpha = jax.nn.softmax(jnp.stack(lses), axis=0)
    o = jnp.sum(alpha[..., None] * jnp.stack(outs), axis=0)
    return o.reshape(o.shape[0], o.shape[1], A_HEADS * A_HEAD_DIM)


def mixer_a_prompt(xn, rel_bias, a_w_in, a_w_out):
    B, S, _ = xn.shape
    qkv = (xn @ a_w_in).reshape(B, S, 3, A_GROUPS, A_HEADS, A_HEAD_DIM)
    q = qkv[:, :, 0]
    kvs = [qkv[:, :, 1:, g] for g in range(A_GROUPS)]

    def block(t0):
        qb = lax.dynamic_slice_in_dim(q, t0, Q_BLOCK, axis=1)
        qidx = t0 + jnp.arange(Q_BLOCK, dtype=jnp.int32)
        return dilated_mixture(qb, kvs, [qidx] * A_GROUPS, rel_bias)

    o = map_query_blocks(block, S)
    y = o.astype(xn.dtype) @ a_w_out
    bufs = [kv[:, S - min(w, S):] for kv, w in zip(kvs, A_WINDOWS)]
    return y, bufs


def mixer_a_sample(xn, bufs, rel_bias, a_w_in, a_w_out):
    B, T, _ = xn.shape
    qkv = (xn @ a_w_in).reshape(B, T, 3, A_GROUPS, A_HEADS, A_HEAD_DIM)
    fulls = [jnp.concatenate([buf, qkv[:, :, 1:, g].astype(buf.dtype)], axis=1) for g, buf in enumerate(bufs)]
    qidxs = [buf.shape[1] + jnp.arange(T, dtype=jnp.int32) for buf in bufs]
    o = dilated_mixture(qkv[:, :, 0], fulls, qidxs, rel_bias)
    y = o.astype(xn.dtype) @ a_w_out
    new_bufs = [f[:, max(0, f.shape[1] - w):] for f, w in zip(fulls, A_WINDOWS)]
    return y, new_bufs


def conformer_conv(xn, conv_state, w_pw1, b_pw1, w_dw, b_dw, ln_g, ln_b, w_pw2, b_pw2):
    h = xn @ w_pw1 + b_pw1
    u = h[..., :D_MODEL] * jax.nn.sigmoid(h[..., D_MODEL:])
    up = jnp.concatenate([conv_state.astype(u.dtype), u], axis=1)
    c = lax.conv_general_dilated(up, w_dw[:, None, :].astype(up.dtype), (1,), 'VALID',
                                 dimension_numbers=('NWC', 'WIO', 'NWC'),
                                 feature_group_count=D_MODEL) + b_dw
    c = jax.nn.silu(layernorm(c, ln_g, ln_b))
    y = c @ w_pw2 + b_pw2
    return y, up[:, -(B_CONV_WIDTH - 1):]


def rope_cos_sin(pos):
    inv = ROPE_THETA ** (-jnp.arange(0, C_ROPE, 2, dtype=jnp.float32) / C_ROPE)
    ang = pos.astype(jnp.float32)[:, None] * inv[None, :]
    return jnp.cos(ang), jnp.sin(ang)


def apply_rope(x, cos, sin):
    xf = x.astype(jnp.float32)
    x1, x2 = jnp.split(xf, 2, axis=-1)
    return jnp.concatenate([x1 * cos - x2 * sin, x2 * cos + x1 * sin], axis=-1).astype(x.dtype)


def mla_project(xn, pos, c_w_down, c_g_q, c_g_kv, c_w_uq):
    B, T, _ = xn.shape
    h = xn @ c_w_down
    cq = rmsnorm(h[..., :C_Q_LORA], c_g_q)
    latent = rmsnorm(h[..., C_Q_LORA:C_Q_LORA + C_KV_LORA], c_g_kv)
    cos, sin = rope_cos_sin(pos)
    k_rope = apply_rope(h[..., C_Q_LORA + C_KV_LORA:], cos, sin)
    q = (cq @ c_w_uq).reshape(B, T, C_HEADS, C_NOPE + C_ROPE)
    q_rope = apply_rope(q[..., C_NOPE:], cos[:, None], sin[:, None])
    return q[..., :C_NOPE], q_rope, latent, k_rope


def mla_attend(q_nope, q_rope, qpos, latent, k_rope, c_w_uk, c_w_uv):
    L = latent.shape[1]
    q_lat = jnp.einsum('bqhn,chn->bqhc', q_nope, c_w_uk)
    s = (jnp.einsum('bqhc,blc->bhql', q_lat, latent).astype(jnp.float32)
         + jnp.einsum('bqhr,blr->bhql', q_rope, k_rope).astype(jnp.float32)) * C_SCALE
    mask = jnp.arange(L, dtype=jnp.int32)[None, :] <= qpos[:, None]
    p = jax.nn.softmax(jnp.where(mask[None, None], s, -jnp.inf), axis=-1)
    o_lat = jnp.einsum('bhql,blc->bqhc', p, latent.astype(jnp.float32))
    o = jnp.einsum('bqhc,chv->bqhv', o_lat, c_w_uv.astype(jnp.float32))
    return o.reshape(o.shape[0], o.shape[1], C_HEADS * C_V)


def mixer_c_prompt(xn, c_w_down, c_g_q, c_g_kv, c_w_uq, c_w_uk, c_w_uv, c_w_out):
    B, S, _ = xn.shape
    q_nope, q_rope, latent, k_rope = mla_project(xn, jnp.arange(S, dtype=jnp.int32), c_w_down, c_g_q, c_g_kv, c_w_uq)

    def block(t0):
        qn = lax.dynamic_slice_in_dim(q_nope, t0, Q_BLOCK, axis=1)
        qr = lax.dynamic_slice_in_dim(q_rope, t0, Q_BLOCK, axis=1)
        return mla_attend(qn, qr, t0 + jnp.arange(Q_BLOCK, dtype=jnp.int32), latent, k_rope, c_w_uk, c_w_uv)

    o = map_query_blocks(block, S)
    return o.astype(xn.dtype) @ c_w_out, latent, k_rope


def mixer_c_sample(xn, cache_c_latent, cache_c_krope, page_table, c_w_down, c_g_q, c_g_kv, c_w_uq, c_w_uk, c_w_uv, c_w_out):
    B, T, _ = xn.shape
    past = page_table.shape[1] * PAGE_SIZE
    pos = past + jnp.arange(T, dtype=jnp.int32)
    q_nope, q_rope, latent, k_rope = mla_project(xn, pos, c_w_down, c_g_q, c_g_kv, c_w_uq)
    lat_all = jnp.concatenate([cache_c_latent[page_table].reshape(B, past, C_KV_LORA), latent.astype(cache_c_latent.dtype)], axis=1)
    rope_all = jnp.concatenate([cache_c_krope[page_table].reshape(B, past, C_ROPE), k_rope.astype(cache_c_krope.dtype)], axis=1)
    o = mla_attend(q_nope, q_rope, pos, lat_all, rope_all, c_w_uk, c_w_uv)
    return o.astype(xn.dtype) @ c_w_out, latent, k_rope


def dsa_project(xn, d_w_in):
    B, T, _ = xn.shape
    h = xn @ d_w_in
    o = D_OFFS
    q = h[..., o[0]:o[1]].reshape(B, T, D_HEADS, D_HEAD_DIM)
    kv = h[..., o[1]:o[2]].reshape(B, T, 2, D_KV_HEADS, D_HEAD_DIM)
    qi = h[..., o[2]:o[3]].reshape(B, T, D_IDX_HEADS, D_IDX_DIM)
    ki = h[..., o[3]:o[4]]
    wi = h[..., o[4]:o[5]]
    return q, kv, qi, ki, wi


def indexer_scores(qi, wi, ki, qpos):
    L = ki.shape[1]
    dots = jnp.einsum('bqhd,bld->bqhl', qi, ki).astype(jnp.float32) * (D_IDX_DIM ** -0.5)
    score = jnp.einsum('bqh,bqhl->bql', wi.astype(jnp.float32) * (D_IDX_HEADS ** -0.5), jax.nn.relu(dots))
    mask = jnp.arange(L, dtype=jnp.int32)[None, :] <= qpos[:, None]
    return jnp.where(mask[None], score, -jnp.inf)


def dsa_attend(q, kv_sel, sel, valid, qpos, rel_bias):
    B, Q = q.shape[:2]
    R = D_HEADS // D_KV_HEADS
    qg = q.reshape(B, Q, D_KV_HEADS, R, D_HEAD_DIM)
    s = jnp.einsum('bqgrd,bqkgd->bqgrk', qg, kv_sel[:, :, :, 0]).astype(jnp.float32) * D_SCALE
    bias = rel_bias[:, :D_HEADS][rel_bucket(qpos[None, :, None] - sel)].astype(jnp.float32)
    bias = jnp.moveaxis(bias.reshape(B, Q, -1, D_KV_HEADS, R), 2, -1)
    s = jnp.where(valid[:, :, None, None, :], s + bias, -jnp.inf)
    p = jax.nn.softmax(s, axis=-1)
    o = jnp.einsum('bqgrk,bqkgd->bqgrd', p, kv_sel[:, :, :, 1].astype(jnp.float32))
    return o.reshape(B, Q, D_HEADS * D_HEAD_DIM)


def mixer_d_prompt(xn, rel_bias, d_w_in, d_w_out):
    B, S, _ = xn.shape
    q, kv, qi, ki, wi = dsa_project(xn, d_w_in)
    k_top = min(D_TOPK_MAX, S // 4)

    def block(t0):
        qpos = t0 + jnp.arange(Q_BLOCK, dtype=jnp.int32)
        qb = lax.dynamic_slice_in_dim(q, t0, Q_BLOCK, axis=1)
        qib = lax.dynamic_slice_in_dim(qi, t0, Q_BLOCK, axis=1)
        wib = lax.dynamic_slice_in_dim(wi, t0, Q_BLOCK, axis=1)
        _, sel = lax.top_k(indexer_scores(qib, wib, ki, qpos), k_top)
        valid = sel <= qpos[None, :, None]
        kv_sel = jax.vmap(lambda a, i: a[i])(kv, sel)
        return dsa_attend(qb, kv_sel, sel, valid, qpos, rel_bias)

    o = map_query_blocks(block, S)
    return o.astype(xn.dtype) @ d_w_out, kv, ki


def mixer_d_sample(xn, cache_d_kv, cache_d_kidx, page_table, rel_bias, d_w_in, d_w_out):
    B, T, _ = xn.shape
    q, kv, qi, ki, wi = dsa_project(xn, d_w_in)
    past = page_table.shape[1] * PAGE_SIZE
    qpos = past + jnp.arange(T, dtype=jnp.int32)
    ki_all = jnp.concatenate([cache_d_kidx[page_table].reshape(B, past, D_IDX_DIM), ki.astype(cache_d_kidx.dtype)], axis=1)
    k_top = min(D_TOPK_MAX, (past + T) // 4)
    _, sel = lax.top_k(indexer_scores(qi, wi, ki_all, qpos), k_top)
    valid = sel <= qpos[None, :, None]
    bidx = jnp.arange(B, dtype=jnp.int32)[:, None, None]
    pidx = jnp.minimum(sel, past - 1)
    phys = page_table[bidx, pidx // PAGE_SIZE]
    kv_past = cache_d_kv[phys, pidx % PAGE_SIZE]
    kv_new = kv[bidx, jnp.clip(sel - past, 0, T - 1)].astype(kv_past.dtype)
    kv_sel = jnp.where((sel < past)[..., None, None, None], kv_past, kv_new)
    o = dsa_attend(q, kv_sel, sel, valid, qpos, rel_bias)
    return o.astype(xn.dtype) @ d_w_out, kv, ki


def cross_attend(xn, mem_kv, w_q, w_o):
    B, T, _ = xn.shape
    q = (xn @ w_q).reshape(B, T, X_HEADS, X_HEAD_DIM)
    s = jnp.einsum('bthd,bmhd->bhtm', q, mem_kv[:, :, 0]).astype(jnp.float32) * X_SCALE
    p = jax.nn.softmax(s, axis=-1)
    o = jnp.einsum('bhtm,bmhd->bthd', p, mem_kv[:, :, 1].astype(jnp.float32))
    return o.reshape(B, T, X_HEADS * X_HEAD_DIM).astype(xn.dtype) @ w_o


def swiglu(xn, w_in, w_out):
    h = xn @ w_in
    return (jax.nn.silu(h[..., :D_FF]) * h[..., D_FF:]) @ w_out


def setup_inputs(seed: int = 0) -> dict:
    key = jax.random.key(seed)
    keys = jax.random.split(key, 64)
    counter = [0]

    def nrm(shape, scale=1.0):
        k = keys[counter[0]]
        counter[0] += 1
        return scale * jax.random.normal(k, shape, jnp.float32)

    def gain(shape):
        return 1.0 + 0.05 * nrm(shape)

    D = D_MODEL
    n_pages = PAST_LEN // PAGE_SIZE
    n_pool = (DEC_BATCH * n_pages * 5) // 4
    a_w = A_HEADS * A_HEAD_DIM
    x_prompt = nrm((BATCH, SEQ, D))
    x_sample = nrm((DEC_BATCH, DEC_SEQ, D))
    mem_prompt = nrm((BATCH, MEM_LEN, D))
    cache_a1_kv = nrm((DEC_BATCH, min(A_WINDOWS[0], PAST_LEN), 2, A_HEADS, A_HEAD_DIM))
    cache_a2_kv = nrm((DEC_BATCH, min(A_WINDOWS[1], PAST_LEN), 2, A_HEADS, A_HEAD_DIM))
    cache_a3_kv = nrm((DEC_BATCH, min(A_WINDOWS[2], PAST_LEN), 2, A_HEADS, A_HEAD_DIM))
    state_b_conv = nrm((DEC_BATCH, B_CONV_WIDTH - 1, D), 0.5)
    cache_c_latent = nrm((n_pool, PAGE_SIZE, C_KV_LORA))
    cache_c_krope = nrm((n_pool, PAGE_SIZE, C_ROPE))
    cache_d_kv = nrm((n_pool, PAGE_SIZE, 2, D_KV_HEADS, D_HEAD_DIM))
    cache_d_kidx = nrm((n_pool, PAGE_SIZE, D_IDX_DIM))
    cache_mem_kv = nrm((DEPTH, DEC_BATCH, MEM_LEN, 2, X_HEADS, X_HEAD_DIM))
    perm = jax.random.permutation(keys[counter[0]], n_pool)
    counter[0] += 1
    page_table = perm[:DEC_BATCH * n_pages].reshape(DEC_BATCH, n_pages).astype(jnp.int32)
    return {
        'x_prompt': x_prompt,
        'x_sample': x_sample,
        'mem_prompt': mem_prompt,
        'cache_a1_kv': cache_a1_kv,
        'cache_a2_kv': cache_a2_kv,
        'cache_a3_kv': cache_a3_kv,
        'state_b_conv': state_b_conv,
        'cache_c_latent': cache_c_latent,
        'cache_c_krope': cache_c_krope,
        'cache_d_kv': cache_d_kv,
        'cache_d_kidx': cache_d_kidx,
        'cache_mem_kv': cache_mem_kv,
        'page_table': page_table,
        'rel_bias': nrm((REL_BUCKETS, REL_COLS), 0.2),
        'g_mix': gain((DEPTH, D)),
        'g_cross': gain((DEPTH, D)),
        'g_ffn': gain((DEPTH, D)),
        'g_final': gain((D,)),
        'w_xq': nrm((DEPTH, D, X_HEADS * X_HEAD_DIM), D ** -0.5),
        'w_xkv': nrm((DEPTH, D, 2 * X_HEADS * X_HEAD_DIM), D ** -0.5),
        'w_xo': nrm((DEPTH, X_HEADS * X_HEAD_DIM, D), (X_HEADS * X_HEAD_DIM) ** -0.5),
        'w_ffn_in': nrm((DEPTH, D, 2 * D_FF), D ** -0.5),
        'w_ffn_out': nrm((DEPTH, D_FF, D), D_FF ** -0.5),
        'a_w_in': nrm((D, 3 * A_GROUPS * a_w), D ** -0.5),
        'a_w_out': nrm((a_w, D), a_w ** -0.5),
        'b_w_pw1': nrm((D, 2 * D), D ** -0.5),
        'b_b_pw1': nrm((2 * D,), 0.02),
        'b_w_dw': nrm((B_CONV_WIDTH, D), B_CONV_WIDTH ** -0.5),
        'b_b_dw': nrm((D,), 0.02),
        'b_ln_g': gain((D,)),
        'b_ln_b': nrm((D,), 0.02),
        'b_w_pw2': nrm((D, D), D ** -0.5),
        'b_b_pw2': nrm((D,), 0.02),
        'c_w_down': nrm((D, C_Q_LORA + C_KV_LORA + C_ROPE), D ** -0.5),
        'c_g_q': gain((C_Q_LORA,)),
        'c_g_kv': gain((C_KV_LORA,)),
        'c_w_uq': nrm((C_Q_LORA, C_HEADS * (C_NOPE + C_ROPE)), C_Q_LORA ** -0.5),
        'c_w_uk': nrm((C_KV_LORA, C_HEADS, C_NOPE), C_KV_LORA ** -0.5),
        'c_w_uv': nrm((C_KV_LORA, C_HEADS, C_V), C_KV_LORA ** -0.5),
        'c_w_out': nrm((C_HEADS * C_V, D), (C_HEADS * C_V) ** -0.5),
        'd_w_in': nrm((D, D_OFFS[-1]), D ** -0.5),
        'd_w_out': nrm((D_HEADS * D_HEAD_DIM, D), (D_HEADS * D_HEAD_DIM) ** -0.5),
    }


def reference(x_prompt, x_sample, mem_prompt, cache_a1_kv, cache_a2_kv, cache_a3_kv, state_b_conv,
              cache_c_latent, cache_c_krope, cache_d_kv, cache_d_kidx, cache_mem_kv, page_table,
              rel_bias, g_mix, g_cross, g_ffn, g_final, w_xq, w_xkv, w_xo, w_ffn_in, w_ffn_out,
              a_w_in, a_w_out, b_w_pw1, b_b_pw1, b_w_dw, b_b_dw, b_ln_g, b_ln_b, b_w_pw2, b_b_pw2,
              c_w_down, c_g_q, c_g_kv, c_w_uq, c_w_uk, c_w_uv, c_w_out, d_w_in, d_w_out):
    xp, xs = x_prompt, x_sample
    B = xp.shape[0]
    mem_kv_prompt = []
    for i in range(DEPTH):
        kind = i % N_MIXERS
        hp = rmsnorm(xp, g_mix[i])
        hs = rmsnorm(xs, g_mix[i])
        if kind == 0:
            yp, a_p = mixer_a_prompt(hp, rel_bias, a_w_in, a_w_out)
            ys, a_s = mixer_a_sample(hs, [cache_a1_kv, cache_a2_kv, cache_a3_kv], rel_bias, a_w_in, a_w_out)
        elif kind == 1:
            zeros_state = jnp.zeros((B, B_CONV_WIDTH - 1, D_MODEL), hp.dtype)
            yp, conv_p = conformer_conv(hp, zeros_state, b_w_pw1, b_b_pw1, b_w_dw, b_b_dw, b_ln_g, b_ln_b, b_w_pw2, b_b_pw2)
            ys, conv_s = conformer_conv(hs, state_b_conv, b_w_pw1, b_b_pw1, b_w_dw, b_b_dw, b_ln_g, b_ln_b, b_w_pw2, b_b_pw2)
        elif kind == 2:
            yp, c_lat_p, c_rope_p = mixer_c_prompt(hp, c_w_down, c_g_q, c_g_kv, c_w_uq, c_w_uk, c_w_uv, c_w_out)
            ys, c_lat_s, c_rope_s = mixer_c_sample(hs, cache_c_latent, cache_c_krope, page_table, c_w_down, c_g_q, c_g_kv, c_w_uq, c_w_uk, c_w_uv, c_w_out)
        else:
            yp, d_kv_p, d_kidx_p = mixer_d_prompt(hp, rel_bias, d_w_in, d_w_out)
            ys, d_kv_s, d_kidx_s = mixer_d_sample(hs, cache_d_kv, cache_d_kidx, page_table, rel_bias, d_w_in, d_w_out)
        xp = xp + yp
        xs = xs + ys
        mkv = (mem_prompt @ w_xkv[i]).reshape(B, MEM_LEN, 2, X_HEADS, X_HEAD_DIM)
        mem_kv_prompt.append(mkv)
        xp = xp + cross_attend(rmsnorm(xp, g_cross[i]), mkv, w_xq[i], w_xo[i])
        xs = xs + cross_attend(rmsnorm(xs, g_cross[i]), cache_mem_kv[i], w_xq[i], w_xo[i])
        xp = xp + swiglu(rmsnorm(xp, g_ffn[i]), w_ffn_in[i], w_ffn_out[i])
        xs = xs + swiglu(rmsnorm(xs, g_ffn[i]), w_ffn_in[i], w_ffn_out[i])
    y_prompt = rmsnorm(xp, g_final)
    y_sample = rmsnorm(xs, g_final)
    new_mem_kv_prompt = jnp.stack(mem_kv_prompt)
    return (y_prompt, y_sample, a_p[0], a_p[1], a_p[2], a_s[0], a_s[1], a_s[2], conv_p, conv_s,
            c_lat_p, c_rope_p, c_lat_s, c_rope_s, d_kv_p, d_kidx_p, d_kv_s, d_kidx_s, new_mem_kv_prompt)
```

```python
import functools
import math

import jax
import jax.numpy as jnp
import numpy as np
from jax import lax
from jax.experimental import pallas as pl
from jax.experimental.pallas import tpu as pltpu

F32 = jnp.float32
BF16 = jnp.bfloat16
I32 = jnp.int32

EPS = 1e-6
PAGE = 128
LANES = 128
VMEM_LIMIT = 52 * 1024 * 1024
NEG = -1e30

REL_BUCKETS = 32
REL_MAX_DIST = 2048
A_WINDOWS = (128, 512, 2048)
A_DILATIONS = (1, 4, 16)
A_HEADS = 8
A_HD = 64
A_NKEYS = 129
B_CONV_WIDTH = 31
C_HEADS = 16
C_Q_LORA = 384
C_KV_LORA = 256
C_NOPE = 64
C_ROPE = 32
C_V = 64
ROPE_THETA = 10000.0
D_HEADS = 16
D_KV_HEADS = 4
D_HD = 64
D_IDX_HEADS = 8
D_IDX_DIM = 64
D_TOPK_MAX = 256
X_HEADS = 4
X_HD = 128
MEM_LEN = 256
Q_TILE = 128


def _cp(*sem):
    return pltpu.CompilerParams(dimension_semantics=sem, vmem_limit_bytes=VMEM_LIMIT)


def _nt(a, b):
    return lax.dot_general(a, b, (((1,), (1,)), ((), ())), preferred_element_type=F32)


def _nn(a, b):
    return jnp.dot(a, b, preferred_element_type=F32)


def _sigmoid(x):
    return 1.0 / (1.0 + jnp.exp(-x))


def _rms(x, g):
    return x * lax.rsqrt(jnp.mean(x * x, axis=-1, keepdims=True) + EPS) * g


def _rel_bucket(dist):
    n = jnp.maximum(dist, 0)
    max_exact = REL_BUCKETS // 2
    nf = jnp.maximum(n, 1).astype(F32)
    large = max_exact + (jnp.log(nf / max_exact) / math.log(REL_MAX_DIST / max_exact)
                         * (REL_BUCKETS - max_exact)).astype(I32)
    large = jnp.minimum(large, REL_BUCKETS - 1)
    return jnp.where(n < max_exact, n, large)


def _mm_body(*refs, norm, bias, glu, resid):
    it = iter(refs)
    x_ref = next(it)
    w_ref = next(it)
    g_ref = next(it) if norm else None
    b_ref = next(it) if bias else None
    w2_ref = next(it) if glu else None
    b2_ref = next(it) if (glu and bias) else None
    r_ref = next(it) if resid else None
    o_ref = next(it)
    xn_ref = next(it)

    @pl.when(pl.program_id(1) == 0)
    def _():
        x = x_ref[...].astype(F32)
        if norm:
            x = _rms(x, g_ref[...])
        xn_ref[...] = x.astype(BF16)

    xn = xn_ref[...]
    h = _nn(xn, w_ref[...])
    if bias:
        h = h + b_ref[...]
    if glu:
        h2 = _nn(xn, w2_ref[...])
        if bias:
            h2 = h2 + b2_ref[...]
        h = h * _sigmoid(h2)
    if resid:
        h = h + r_ref[...]
    o_ref[...] = h.astype(o_ref.dtype)


def _mm(x, w, *, g=None, b=None, w2=None, b2=None, res=None, out_dtype=F32, tm=512, tn=None):
    n, k = x.shape
    m = w.shape[1]
    tm = min(tm, n)
    tn = tn or m
    assert n % tm == 0 and m % tn == 0
    args = [x, w]
    specs = [pl.BlockSpec((tm, k), lambda i, j: (i, 0)),
             pl.BlockSpec((k, tn), lambda i, j: (0, j))]
    if g is not None:
        args.append(g.reshape(1, k))
        specs.append(pl.BlockSpec((1, k), lambda i, j: (0, 0)))
    if b is not None:
        args.append(b.reshape(1, m))
        specs.append(pl.BlockSpec((1, tn), lambda i, j: (0, j)))
    if w2 is not None:
        args.append(w2)
        specs.append(pl.BlockSpec((k, tn), lambda i, j: (0, j)))
        if b2 is not None:
            args.append(b2.reshape(1, m))
            specs.append(pl.BlockSpec((1, tn), lambda i, j: (0, j)))
    if res is not None:
        args.append(res)
        specs.append(pl.BlockSpec((tm, tn), lambda i, j: (i, j)))
    body = functools.partial(_mm_body, norm=g is not None, bias=b is not None,
                             glu=w2 is not None, resid=res is not None)
    return pl.pallas_call(
        body,
        grid=(n // tm, m // tn),
        in_specs=specs,
        out_specs=pl.BlockSpec((tm, tn), lambda i, j: (i, j)),
        out_shape=jax.ShapeDtypeStruct((n, m), out_dtype),
        scratch_shapes=[pltpu.VMEM((tm, k), BF16)],
        compiler_params=_cp("parallel", "arbitrary"),
        name="mm",
    )(*args)


def _ffn_body(x_ref, g_ref, wg_ref, wu_ref, wo_ref, o_ref, xn_ref):
    j = pl.program_id(1)

    @pl.when(j == 0)
    def _():
        x = x_ref[...]
        xn_ref[...] = _rms(x, g_ref[...]).astype(BF16)
        o_ref[...] = x

    xn = xn_ref[...]
    hg = _nn(xn, wg_ref[...])
    hu = _nn(xn, wu_ref[...])
    a = (hg * _sigmoid(hg) * hu).astype(BF16)
    o_ref[...] += _nn(a, wo_ref[...])


def _ffn(x, g, w_in, w_out, *, tm=512):
    n, d = x.shape
    dff = w_out.shape[0]
    tf = dff // 2
    assert tf % LANES == 0
    tm = min(tm, n)
    nf = dff // tf
    return pl.pallas_call(
        _ffn_body,
        grid=(n // tm, nf),
        in_specs=[pl.BlockSpec((tm, d), lambda i, j: (i, 0)),
                  pl.BlockSpec((1, d), lambda i, j: (0, 0)),
                  pl.BlockSpec((d, tf), lambda i, j: (0, j)),
                  pl.BlockSpec((d, tf), lambda i, j: (0, j + nf)),
                  pl.BlockSpec((tf, d), lambda i, j: (j, 0))],
        out_specs=pl.BlockSpec((tm, d), lambda i, j: (i, 0)),
        out_shape=jax.ShapeDtypeStruct((n, d), F32),
        scratch_shapes=[pltpu.VMEM((tm, d), BF16)],
        compiler_params=_cp("parallel", "arbitrary"),
        name="ffn",
    )(x, g.reshape(1, d), w_in, w_in, w_out)


def _cross_body(x_ref, g_ref, wq_ref, wo_ref, kv_ref, o_ref):
    x = x_ref[0]
    xn = _rms(x, g_ref[...]).astype(BF16)
    q = _nn(xn, wq_ref[...]).astype(BF16)
    hw = X_HEADS * X_HD
    outs = []
    for h in range(X_HEADS):
        kh = kv_ref[0, :, h * X_HD:(h + 1) * X_HD].astype(BF16)
        vh = kv_ref[0, :, hw + h * X_HD:hw + (h + 1) * X_HD].astype(BF16)
        s = _nt(q[:, h * X_HD:(h + 1) * X_HD], kh) * (X_HD ** -0.5)
        m = jnp.max(s, axis=-1, keepdims=True)
        p = jnp.exp(s - m)
        l = jnp.sum(p, axis=-1, keepdims=True)
        outs.append(_nn((p / l).astype(BF16), vh))
    o = jnp.concatenate(outs, axis=-1).astype(BF16)
    o_ref[0] = x + _nn(o, wo_ref[...])


def _cross(x3, g, wq, wo, kv3, *, tq=512):
    bsz, t, d = x3.shape
    tq = min(tq, t)
    hw = X_HEADS * X_HD
    return pl.pallas_call(
        _cross_body,
        grid=(bsz, t // tq),
        in_specs=[pl.BlockSpec((1, tq, d), lambda b, i: (b, i, 0)),
                  pl.BlockSpec((1, d), lambda b, i: (0, 0)),
                  pl.BlockSpec((d, hw), lambda b, i: (0, 0)),
                  pl.BlockSpec((hw, d), lambda b, i: (0, 0)),
                  pl.BlockSpec((1, MEM_LEN, 2 * hw), lambda b, i: (b, 0, 0))],
        out_specs=pl.BlockSpec((1, tq, d), lambda b, i: (b, i, 0)),
        out_shape=jax.ShapeDtypeStruct((bsz, t, d), F32),
        compiler_params=_cp("parallel", "parallel"),
        name="cross",
    )(x3, g.reshape(1, d), wq, wo, kv3)


def _rms_body(x_ref, g_ref, o_ref):
    o_ref[...] = _rms(x_ref[...], g_ref[...])


def _rmsnorm(x, g, *, tm=512):
    n, d = x.shape
    tm = min(tm, n)
    return pl.pallas_call(
        _rms_body,
        grid=(n // tm,),
        in_specs=[pl.BlockSpec((tm, d), lambda i: (i, 0)),
                  pl.BlockSpec((1, d), lambda i: (0, 0))],
        out_specs=pl.BlockSpec((tm, d), lambda i: (i, 0)),
        out_shape=jax.ShapeDtypeStruct((n, d), F32),
        compiler_params=_cp("parallel"),
        name="rmsnorm",
    )(x, g.reshape(1, d))


def _a_bias_vec(rel_bias, g):
    dist = A_DILATIONS[g] * jnp.arange(A_NKEYS, dtype=I32)
    return rel_bias[_rel_bucket(dist)][:, g * A_HEADS:(g + 1) * A_HEADS].astype(F32).T


def _a_prompt_body(q_ref, kc_ref, kp_ref, vc_ref, vp_ref, bias_ref, o_ref, lse_ref):
    j = pl.program_id(2)
    q = q_ref[0].astype(BF16)
    k = jnp.concatenate([kp_ref[0], kc_ref[0]], axis=0).astype(BF16)
    v = jnp.concatenate([vp_ref[0], vc_ref[0]], axis=0).astype(BF16)
    t = Q_TILE
    row = lax.broadcasted_iota(I32, (t, 2 * t), 0)
    col = lax.broadcasted_iota(I32, (t, 2 * t), 1)
    back = row + t - col
    ok = (back >= 0) & (back <= t) & ((col >= t) | (j > 0))
    outs, lses = [], []
    for h in range(A_HEADS):
        sl = slice(h * A_HD, (h + 1) * A_HD)
        s = _nt(q[:, sl], k[:, sl]) * (A_HD ** -0.5) + bias_ref[h]
        s = jnp.where(ok, s, NEG)
        m = jnp.max(s, axis=-1, keepdims=True)
        p = jnp.exp(s - m)
        l = jnp.sum(p, axis=-1, keepdims=True)
        outs.append(_nn(p.astype(BF16), v[:, sl]) / l)
        lses.append(jnp.broadcast_to(m + jnp.log(l), (t, A_HD)))
    o_ref[0] = jnp.concatenate(outs, axis=-1)
    lse_ref[0] = jnp.concatenate(lses, axis=-1)


def _a_prompt_group(qkv3, bias_tile, g):
    bsz, s, c = qkv3.shape
    d = A_DILATIONS[g]
    hw = A_HEADS * A_HD
    nblk = c // hw
    t = Q_TILE
    sj = s // d
    assert sj % t == 0
    x = qkv3.reshape(bsz, sj, d * c)

    def cur(which):
        return pl.BlockSpec((1, t, hw), lambda b, r, j: (b, j, r * nblk + 3 * which + g))

    def prev(which):
        return pl.BlockSpec((1, t, hw),
                            lambda b, r, j: (b, jnp.maximum(j - 1, 0), r * nblk + 3 * which + g))

    out_spec = pl.BlockSpec((1, t, hw), lambda b, r, j: (b, j, r))
    o, lse = pl.pallas_call(
        _a_prompt_body,
        grid=(bsz, d, sj // t),
        in_specs=[cur(0), cur(1), prev(1), cur(2), prev(2),
                  pl.BlockSpec((A_HEADS, t, 2 * t), lambda b, r, j: (0, 0, 0))],
        out_specs=[out_spec, out_spec],
        out_shape=[jax.ShapeDtypeStruct((bsz, sj, d * hw), F32)] * 2,
        compiler_params=_cp("parallel", "parallel", "parallel"),
        name="a_prompt",
    )(x, x, x, x, x, bias_tile)
    return o.reshape(bsz, s, hw), lse.reshape(bsz, s, hw)


def _a_bias_tile(vec):
    t = Q_TILE
    back = jnp.arange(t)[:, None] + t - jnp.arange(2 * t)[None, :]
    ok = (back >= 0) & (back <= t)
    return jnp.where(ok[None], vec[:, jnp.clip(back, 0, t)], 0.0)


def _a_combine_body(o0, o1, o2, l0, l1, l2, x_ref, w_ref, out_ref):
    a0, a1, a2 = l0[...], l1[...], l2[...]
    m = jnp.maximum(jnp.maximum(a0, a1), a2)
    e0, e1, e2 = jnp.exp(a0 - m), jnp.exp(a1 - m), jnp.exp(a2 - m)
    den = e0 + e1 + e2
    o = (e0 / den) * o0[...] + (e1 / den) * o1[...] + (e2 / den) * o2[...]
    out_ref[...] = x_ref[...] + _nn(o.astype(BF16), w_ref[...])


def _a_combine(os_, ls_, x, w_out, *, tm=512):
    n, d = x.shape
    hw = w_out.shape[0]
    tm = min(tm, n)
    small = pl.BlockSpec((tm, hw), lambda i: (i, 0))
    return pl.pallas_call(
        _a_combine_body,
        grid=(n // tm,),
        in_specs=[small] * 6 + [pl.BlockSpec((tm, d), lambda i: (i, 0)),
                                pl.BlockSpec((hw, d), lambda i: (0, 0))],
        out_specs=pl.BlockSpec((tm, d), lambda i: (i, 0)),
        out_shape=jax.ShapeDtypeStruct((n, d), F32),
        compiler_params=_cp("parallel"),
        name="a_combine",
    )(*os_, *ls_, x, w_out)


def _row_to_col(r):
    n = r.shape[1]
    eye = (lax.broadcasted_iota(I32, (LANES, LANES), 0) == lax.broadcasted_iota(I32, (LANES, LANES), 1))
    cols = []
    for c in range(n // LANES):
        blk = jnp.broadcast_to(r[:, c * LANES:(c + 1) * LANES], (LANES, LANES))
        cols.append(jnp.sum(jnp.where(eye, blk, 0.0), axis=1, keepdims=True))
    return jnp.concatenate(cols, axis=0)


def _col_to_row(c):
    n = c.shape[0]
    eye = (lax.broadcasted_iota(I32, (LANES, LANES), 0) == lax.broadcasted_iota(I32, (LANES, LANES), 1))
    rows = []
    for i in range(n // LANES):
        blk = jnp.broadcast_to(c[i * LANES:(i + 1) * LANES, :], (LANES, LANES))
        rows.append(jnp.sum(jnp.where(eye, blk, 0.0), axis=0, keepdims=True))
    return jnp.concatenate(rows, axis=1)


def _a_sample_body(q_ref, kvn_ref, bias_ref, c_ref, o_ref, lse_ref, cn_ref, *, width):
    hw = A_HEADS * A_HD
    q_col = _row_to_col(q_ref[0])
    kvn_row = kvn_ref[0]
    kvn_col = _row_to_col(kvn_row)
    scale = A_HD ** -0.5
    s_rows, self_rows = [], []
    for h in range(A_HEADS):
        sl = slice(h * A_HD, (h + 1) * A_HD)
        kt = c_ref[0, sl, :]
        s_rows.append(jnp.sum(kt * q_col[sl], axis=0, keepdims=True))
        self_rows.append(jnp.sum(kvn_col[sl] * q_col[sl], axis=0, keepdims=True))
    bias = bias_ref[...]
    s = jnp.concatenate(s_rows, axis=0) * scale + bias[:, :width]
    s_self = jnp.concatenate(self_rows, axis=0) * scale + bias[:, width:width + 1]
    m = jnp.maximum(jnp.max(s, axis=-1, keepdims=True), s_self)
    p = jnp.exp(s - m)
    p_self = jnp.exp(s_self - m)
    l = jnp.sum(p, axis=-1, keepdims=True) + p_self
    lse = m + jnp.log(l)
    o_cols, lse_cols = [], []
    for h in range(A_HEADS):
        sl = slice(hw + h * A_HD, hw + (h + 1) * A_HD)
        vt = c_ref[0, sl, :]
        oc = jnp.sum(vt * p[h:h + 1, :], axis=1, keepdims=True) + kvn_col[sl] * p_self[h:h + 1, :]
        o_cols.append(oc / l[h:h + 1, :])
        lse_cols.append(jnp.broadcast_to(lse[h:h + 1, :], (A_HD, 1)))
    o_ref[0] = _col_to_row(jnp.concatenate(o_cols, axis=0))
    lse_ref[0] = _col_to_row(jnp.concatenate(lse_cols, axis=0))
    rc = 128
    lane = lax.broadcasted_iota(I32, (rc, width), 1)
    for r0 in range(0, 2 * hw, rc):
        rolled = pltpu.roll(c_ref[0, r0:r0 + rc, :], width - 1, 1)
        cn_ref[0, r0:r0 + rc, :] = jnp.where(lane == width - 1, kvn_col[r0:r0 + rc], rolled)


def _a_sample_group(q_g, kvn_g, cache_t, bias_s):
    bsz, rows, width = cache_t.shape
    hw = A_HEADS * A_HD
    body = functools.partial(_a_sample_body, width=width)
    return pl.pallas_call(
        body,
        grid=(bsz,),
        in_specs=[pl.BlockSpec((1, 1, hw), lambda b: (b, 0, 0)),
                  pl.BlockSpec((1, 1, 2 * hw), lambda b: (b, 0, 0)),
                  pl.BlockSpec((A_HEADS, width + LANES), lambda b: (0, 0)),
                  pl.BlockSpec((1, rows, width), lambda b: (b, 0, 0))],
        out_specs=[pl.BlockSpec((1, 1, hw), lambda b: (b, 0, 0)),
                   pl.BlockSpec((1, 1, hw), lambda b: (b, 0, 0)),
                   pl.BlockSpec((1, rows, width), lambda b: (b, 0, 0))],
        out_shape=[jax.ShapeDtypeStruct((bsz, 1, hw), F32),
                   jax.ShapeDtypeStruct((bsz, 1, hw), F32),
                   jax.ShapeDtypeStruct((bsz, rows, width), F32)],
        compiler_params=_cp("parallel"),
        name="a_sample",
    )(q_g, kvn_g, bias_s, cache_t)


def _mixer_a(xp, xs, bsz, seq, caches, rel_bias, g, a_w_in, a_w_out):
    hw = A_HEADS * A_HD
    w_in = a_w_in.astype(BF16)
    w_out = a_w_out.astype(BF16)
    qkv_p = _mm(xp, w_in, g=g, tn=w_in.shape[1] // 2)
    qkv_s = _mm(xs, w_in, g=g, tn=w_in.shape[1] // 2)
    qkv3 = qkv_p.reshape(bsz, seq, -1)
    os_p, ls_p, os_s, ls_s, new_p, new_s = [], [], [], [], [], []
    nb = xs.shape[0]
    for gi in range(3):
        vec = _a_bias_vec(rel_bias, gi)
        o, lse = _a_prompt_group(qkv3, _a_bias_tile(vec), gi)
        os_p.append(o.reshape(bsz * seq, hw))
        ls_p.append(lse.reshape(bsz * seq, hw))
        k_p = qkv3[:, :, (3 + gi) * hw:(4 + gi) * hw]
        v_p = qkv3[:, :, (6 + gi) * hw:(7 + gi) * hw]
        w = min(A_WINDOWS[gi], seq)
        kv_p = jnp.stack([k_p[:, seq - w:], v_p[:, seq - w:]], axis=2)
        new_p.append(kv_p.reshape(bsz, w, 2, A_HEADS, A_HD))
        cache = caches[gi]
        width = cache.shape[1]
        d = A_DILATIONS[gi]
        assert width == (A_NKEYS - 1) * d
        cache_t = jnp.transpose(cache, (0, 2, 3, 4, 1)).reshape(nb, 2 * hw, width)
        q_s = qkv_s[:, gi * hw:(gi + 1) * hw].reshape(nb, 1, hw)
        kvn = jnp.concatenate([qkv_s[:, (3 + gi) * hw:(4 + gi) * hw],
                               qkv_s[:, (6 + gi) * hw:(7 + gi) * hw]], axis=1).reshape(nb, 1, 2 * hw)
        lane = jnp.arange(width)
        kk = (width - lane) // d
        bias_c = jnp.where((lane % d == 0)[None, :], vec[:, jnp.clip(kk, 0, A_NKEYS - 1)], NEG)
        bias_s = jnp.concatenate([bias_c, jnp.broadcast_to(vec[:, 0:1], (A_HEADS, LANES))], axis=1)
        o_s, lse_s, cache_new = _a_sample_group(q_s, kvn, cache_t, bias_s)
        os_s.append(o_s.reshape(nb, hw))
        ls_s.append(lse_s.reshape(nb, hw))
        new_s.append(jnp.transpose(cache_new.reshape(nb, 2, A_HEADS, A_HD, width), (0, 4, 1, 2, 3)))
    xp = _a_combine(os_p, ls_p, xp, w_out)
    xs = _a_combine(os_s, ls_s, xs, w_out)
    return xp, xs, new_p, new_s


def _ln_silu_proj(c, lng_ref, lnb_ref, w2_ref, b2_ref, x):
    mu = jnp.mean(c, axis=-1, keepdims=True)
    xc = c - mu
    y = xc * lax.rsqrt(jnp.mean(xc * xc, axis=-1, keepdims=True) + EPS)
    y = y * lng_ref[...] + lnb_ref[...]
    y = y * _sigmoid(y)
    return x + _nn(y.astype(BF16), w2_ref[...]) + b2_ref[...]


def _b_prompt_body(uc_ref, up_ref, x_ref, wdw_ref, bdw_ref, lng_ref, lnb_ref, w2_ref, b2_ref,
                   o_ref, buf_ref, c_ref, *, tq):
    i = pl.program_id(1)
    pad = 32
    buf_ref[0:pad, :] = jnp.where(i > 0, up_ref[0], 0.0)
    buf_ref[pad:pad + tq, :] = uc_ref[0]
    off = pad - (B_CONV_WIDTH - 1)
    rc, cc = 128, 256
    d = uc_ref.shape[2]
    for r0 in range(0, tq, rc):
        for c0 in range(0, d, cc):
            acc = jnp.zeros((rc, cc), F32)
            for w in range(B_CONV_WIDTH):
                acc = acc + buf_ref[r0 + off + w:r0 + off + w + rc, c0:c0 + cc] * wdw_ref[w:w + 1, c0:c0 + cc]
            c_ref[r0:r0 + rc, c0:c0 + cc] = acc
    c = c_ref[...] + bdw_ref[...]
    o_ref[0] = _ln_silu_proj(c, lng_ref, lnb_ref, w2_ref, b2_ref, x_ref[0])


def _b_prompt(u3, x3, wdw, bdw, lng, lnb, w2, b2, *, tq=256):
    bsz, s, d = u3.shape
    tq = min(tq, s)
    pad = 32
    body = functools.partial(_b_prompt_body, tq=tq)
    vec = lambda: pl.BlockSpec((1, d), lambda b, i: (0, 0))
    return pl.pallas_call(
        body,
        grid=(bsz, s // tq),
        in_specs=[pl.BlockSpec((1, tq, d), lambda b, i: (b, i, 0)),
                  pl.BlockSpec((1, pad, d), lambda b, i: (b, jnp.maximum(i * (tq // pad) - 1, 0), 0)),
                  pl.BlockSpec((1, tq, d), lambda b, i: (b, i, 0)),
                  pl.BlockSpec((B_CONV_WIDTH, d), lambda b, i: (0, 0)),
                  vec(), vec(), vec(),
                  pl.BlockSpec((d, d), lambda b, i: (0, 0)),
                  vec()],
        out_specs=pl.BlockSpec((1, tq, d), lambda b, i: (b, i, 0)),
        out_shape=jax.ShapeDtypeStruct((bsz, s, d), F32),
        scratch_shapes=[pltpu.VMEM((tq + pad, d), F32), pltpu.VMEM((tq, d), F32)],
        compiler_params=_cp("parallel", "parallel"),
        name="b_prompt",
    )(u3, u3, x3, wdw, bdw.reshape(1, d), lng.reshape(1, d), lnb.reshape(1, d), w2, b2.reshape(1, d))


def _b_sample_body(st_ref, u_ref, x_ref, wdw_ref, bdw_ref, lng_ref, lnb_ref, w2_ref, b2_ref,
                   o_ref, ns_ref):
    nw = B_CONV_WIDTH - 1
    u = u_ref[...]
    acc = u * wdw_ref[nw:nw + 1, :] + bdw_ref[...]
    for w in range(nw):
        acc = acc + st_ref[w] * wdw_ref[w:w + 1, :]
        if w > 0:
            ns_ref[w - 1] = st_ref[w]
    ns_ref[nw - 1] = u
    o_ref[...] = _ln_silu_proj(acc, lng_ref, lnb_ref, w2_ref, b2_ref, x_ref[...])


def _b_sample(state_t, u, x, wdw, bdw, lng, lnb, w2, b2, *, tb=32):
    nw, nb, d = state_t.shape
    tb = min(tb, nb)
    vec = lambda: pl.BlockSpec((1, d), lambda i: (0, 0))
    return pl.pallas_call(
        _b_sample_body,
        grid=(nb // tb,),
        in_specs=[pl.BlockSpec((nw, tb, d), lambda i: (0, i, 0)),
                  pl.BlockSpec((tb, d), lambda i: (i, 0)),
                  pl.BlockSpec((tb, d), lambda i: (i, 0)),
                  pl.BlockSpec((B_CONV_WIDTH, d), lambda i: (0, 0)),
                  vec(), vec(), vec(),
                  pl.BlockSpec((d, d), lambda i: (0, 0)),
                  vec()],
        out_specs=[pl.BlockSpec((tb, d), lambda i: (i, 0)),
                   pl.BlockSpec((nw, tb, d), lambda i: (0, i, 0))],
        out_shape=[jax.ShapeDtypeStruct((nb, d), F32),
                   jax.ShapeDtypeStruct((nw, nb, d), F32)],
        compiler_params=_cp("parallel"),
        name="b_sample",
    )(state_t, u, x, wdw, bdw.reshape(1, d), lng.reshape(1, d), lnb.reshape(1, d), w2, b2.reshape(1, d))


def _mixer_b(xp, xs, bsz, seq, state, g, w_pw1, b_pw1, w_dw, b_dw, ln_g, ln_b, w_pw2, b_pw2):
    d = xp.shape[1]
    wa = w_pw1[:, :d].astype(BF16)
    wb = w_pw1[:, d:].astype(BF16)
    w2 = w_pw2.astype(BF16)
    nw = B_CONV_WIDTH - 1
    u_p = _mm(xp, wa, g=g, b=b_pw1[:d], w2=wb, b2=b_pw1[d:])
    u_s = _mm(xs, wa, g=g, b=b_pw1[:d], w2=wb, b2=b_pw1[d:])
    u3 = u_p.reshape(bsz, seq, d)
    xp = _b_prompt(u3, xp.reshape(bsz, seq, d), w_dw, b_dw, ln_g, ln_b, w2, b_pw2).reshape(bsz * seq, d)
    conv_p = u3[:, seq - nw:]
    xs, ns_t = _b_sample(jnp.transpose(state, (1, 0, 2)), u_s, xs, w_dw, b_dw, ln_g, ln_b, w2, b_pw2)
    return xp, xs, conv_p, jnp.transpose(ns_t, (1, 0, 2))


C_QW = 128
C_DOWN = C_Q_LORA + C_KV_LORA


def _c_prep_body(h_ref, ct_ref, st_ref, gq_ref, gkv_ref, wqa_ref, wqb_ref, wuk_ref, wuv_ref,
                 q_ref, k_ref, v_ref, lat_ref, kr_ref):
    h = h_ref[...]
    ct = ct_ref[...]
    st = st_ref[...]
    cq = _rms(h[:, :C_Q_LORA], gq_ref[...]).astype(BF16)
    lat = _rms(h[:, C_Q_LORA:C_DOWN], gkv_ref[...])
    lat_ref[...] = lat
    lat_b = lat.astype(BF16)
    krp = h[:, C_DOWN:C_DOWN + C_QW] * ct + h[:, C_DOWN + C_QW:C_DOWN + 2 * C_QW] * st
    kr_ref[...] = krp[:, C_NOPE:C_NOPE + C_ROPE]
    qa = _nn(cq, wqa_ref[...])
    qb = _nn(cq, wqb_ref[...])
    kn = _nn(lat_b, wuk_ref[...])
    for hd in range(C_HEADS):
        sl = slice(hd * C_QW, (hd + 1) * C_QW)
        q_ref[:, sl] = (qa[:, sl] * ct + qb[:, sl] * st).astype(BF16)
        k_ref[:, sl] = (kn[:, sl] + krp).astype(BF16)
    v_ref[...] = _nn(lat_b, wuv_ref[...]).astype(BF16)


def _c_prep(h, ct, st, gq, gkv, wqa, wqb, wuk, wuv, *, tm=512):
    n, hc = h.shape
    tm = min(tm, n)
    nt = ct.shape[0] // tm
    qw = C_HEADS * C_QW
    vw = C_HEADS * C_V
    full = lambda a: pl.BlockSpec(a.shape, lambda i: (0, 0))
    return pl.pallas_call(
        _c_prep_body,
        grid=(n // tm,),
        in_specs=[pl.BlockSpec((tm, hc), lambda i: (i, 0)),
                  pl.BlockSpec((tm, C_QW), lambda i: (i % nt, 0)),
                  pl.BlockSpec((tm, C_QW), lambda i: (i % nt, 0)),
                  full(gq), full(gkv), full(wqa), full(wqb), full(wuk), full(wuv)],
        out_specs=[pl.BlockSpec((tm, qw), lambda i: (i, 0)),
                   pl.BlockSpec((tm, qw), lambda i: (i, 0)),
                   pl.BlockSpec((tm, vw), lambda i: (i, 0)),
                   pl.BlockSpec((tm, C_KV_LORA), lambda i: (i, 0)),
                   pl.BlockSpec((tm, C_ROPE), lambda i: (i, 0))],
        out_shape=[jax.ShapeDtypeStruct((n, qw), BF16),
                   jax.ShapeDtypeStruct((n, qw), BF16),
                   jax.ShapeDtypeStruct((n, vw), BF16),
                   jax.ShapeDtypeStruct((n, C_KV_LORA), F32),
                   jax.ShapeDtypeStruct((n, C_ROPE), F32)],
        compiler_params=_cp("parallel"),
        name="c_prep",
    )(h, ct, st, gq, gkv, wqa, wqb, wuk, wuv)


def _c_flash_body(q_ref, k_ref, v_ref, o_ref, m_ref, l_ref, acc_ref, *, tq, tk, scale):
    qi = pl.program_id(2)
    kj = pl.program_id(3)

    @pl.when(kj == 0)
    def _():
        m_ref[...] = jnp.full(m_ref.shape, NEG, F32)
        l_ref[...] = jnp.zeros(l_ref.shape, F32)
        acc_ref[...] = jnp.zeros(acc_ref.shape, F32)

    @pl.when(kj * tk <= qi * tq + tq - 1)
    def _():
        v = v_ref[0]
        row = qi * tq + lax.broadcasted_iota(I32, (tq, tk), 0)
        col = kj * tk + lax.broadcasted_iota(I32, (tq, tk), 1)
        ok = col <= row
        lane = lax.broadcasted_iota(I32, (tq, 2 * C_V), 1)
        pv, alphas = [], []
        for hh in range(2):
            sl = slice(hh * C_QW, (hh + 1) * C_QW)
            s = _nt(q_ref[0, :, sl], k_ref[0, :, sl]) * scale
            s = jnp.where(ok, s, NEG)
            m_old = m_ref[hh]
            m_new = jnp.maximum(m_old, jnp.max(s, axis=-1, keepdims=True))
            alpha = jnp.exp(m_old - m_new)
            p = jnp.exp(s - m_new)
            l_ref[hh] = alpha * l_ref[hh] + jnp.sum(p, axis=-1, keepdims=True)
            m_ref[hh] = m_new
            pv.append(_nn(p.astype(BF16), v))
            alphas.append(alpha)
        first = lane < C_V
        acc_ref[...] = (jnp.where(first, alphas[0], alphas[1]) * acc_ref[...]
                        + jnp.where(first, pv[0], pv[1]))

    @pl.when(kj == pl.num_programs(3) - 1)
    def _():
        lane = lax.broadcasted_iota(I32, (tq, 2 * C_V), 1)
        l = jnp.where(lane < C_V, l_ref[0], l_ref[1])
        o_ref[0] = (acc_ref[...] / l).astype(o_ref.dtype)


def _c_flash(q3, k3, v3, *, tq=512, tk=512):
    bsz, s, _ = q3.shape
    tq = min(tq, s)
    tk = min(tk, s)
    npair = C_HEADS // 2
    scale = (C_NOPE + C_ROPE) ** -0.5
    body = functools.partial(_c_flash_body, tq=tq, tk=tk, scale=scale)

    def kmap(b, hp, i, j):
        return (b, jnp.minimum(j, (i * tq + tq - 1) // tk), hp)

    return pl.pallas_call(
        body,
        grid=(bsz, npair, s // tq, s // tk),
        in_specs=[pl.BlockSpec((1, tq, 2 * C_QW), lambda b, hp, i, j: (b, i, hp)),
                  pl.BlockSpec((1, tk, 2 * C_QW), kmap),
                  pl.BlockSpec((1, tk, 2 * C_V), kmap)],
        out_specs=pl.BlockSpec((1, tq, 2 * C_V), lambda b, hp, i, j: (b, i, hp)),
        out_shape=jax.ShapeDtypeStruct((bsz, s, C_HEADS * C_V), BF16),
        scratch_shapes=[pltpu.VMEM((2, tq, 1), F32), pltpu.VMEM((2, tq, 1), F32),
                        pltpu.VMEM((tq, 2 * C_V), F32)],
        compiler_params=_cp("parallel", "parallel", "parallel", "arbitrary"),
        name="c_flash",
    )(q3, k3, v3)


def _head_mm_body(x_ref, w_ref, o_ref):
    o_ref[...] = _nn(x_ref[...], w_ref[0]).astype(o_ref.dtype)


def _head_mm(x, w3, *, out_dtype=F32):
    n = x.shape[0]
    nh, kin, kout = w3.shape
    return pl.pallas_call(
        _head_mm_body,
        grid=(nh,),
        in_specs=[pl.BlockSpec((n, kin), lambda h: (0, h)),
                  pl.BlockSpec((1, kin, kout), lambda h: (h, 0, 0))],
        out_specs=pl.BlockSpec((n, kout), lambda h: (0, h)),
        out_shape=jax.ShapeDtypeStruct((n, nh * kout), out_dtype),
        compiler_params=_cp("parallel"),
        name="head_mm",
    )(x, w3)


C_PG = 8


def _c_decode_body(pt_ref, *refs, scale):
    lat_refs = refs[:C_PG]
    rope_refs = refs[C_PG:2 * C_PG]
    ql_ref, qr_ref, latn_ref, krn_ref, o_ref, m_ref, l_ref, acc_ref = refs[2 * C_PG:]
    j = pl.program_id(1)

    @pl.when(j == 0)
    def _():
        m_ref[...] = jnp.full(m_ref.shape, NEG, F32)
        l_ref[...] = jnp.zeros(l_ref.shape, F32)
        acc_ref[...] = jnp.zeros(acc_ref.shape, F32)

    ql = ql_ref[0]
    qr = qr_ref[0]
    lat = jnp.concatenate([r[0] for r in lat_refs], axis=0).astype(BF16)
    s_rope = jnp.concatenate([_nn(qr, r[0].astype(BF16)) for r in rope_refs], axis=1)
    s = (_nt(ql, lat) + s_rope) * scale
    m_old = m_ref[...]
    m_new = jnp.maximum(m_old, jnp.max(s, axis=-1, keepdims=True))
    alpha = jnp.exp(m_old - m_new)
    p = jnp.exp(s - m_new)
    l_ref[...] = alpha * l_ref[...] + jnp.sum(p, axis=-1, keepdims=True)
    acc_ref[...] = alpha * acc_ref[...] + _nn(p.astype(BF16), lat)
    m_ref[...] = m_new

    @pl.when(j == pl.num_programs(1) - 1)
    def _():
        latn = latn_ref[0].astype(BF16)
        krn = krn_ref[0].astype(BF16)
        s_self = (jnp.sum(ql.astype(F32) * latn.astype(F32), axis=1, keepdims=True)
                  + jnp.sum(qr.astype(F32) * krn.astype(F32), axis=1, keepdims=True)) * scale
        m_o = m_ref[...]
        m_n = jnp.maximum(m_o, s_self)
        a = jnp.exp(m_o - m_n)
        p_s = jnp.exp(s_self - m_n)
        l = a * l_ref[...] + p_s
        acc = a * acc_ref[...] + p_s.astype(BF16).astype(F32) * latn.astype(F32)
        o_ref[0] = (acc / l).astype(o_ref.dtype)


def _c_decode(page_flat, n_pages, lat_cache, rope_cache_t, ql3, qr3, latn3, krn3):
    nb = ql3.shape[0]
    scale = (C_NOPE + C_ROPE) ** -0.5
    steps = n_pages // C_PG

    def lat_spec(i):
        return pl.BlockSpec((1, PAGE, C_KV_LORA), lambda b, j, pt: (pt[b * n_pages + j * C_PG + i], 0, 0))

    def rope_spec(i):
        return pl.BlockSpec((1, C_ROPE, PAGE), lambda b, j, pt: (pt[b * n_pages + j * C_PG + i], 0, 0))

    per = lambda shp: pl.BlockSpec((1,) + shp, lambda b, j, pt: (b, 0, 0))
    gs = pltpu.PrefetchScalarGridSpec(
        num_scalar_prefetch=1,
        grid=(nb, steps),
        in_specs=[lat_spec(i) for i in range(C_PG)] + [rope_spec(i) for i in range(C_PG)]
        + [per((C_HEADS, C_KV_LORA)), per((C_HEADS, C_ROPE)), per((1, C_KV_LORA)), per((1, C_ROPE))],
        out_specs=per((C_HEADS, C_KV_LORA)),
        scratch_shapes=[pltpu.VMEM((C_HEADS, 1), F32), pltpu.VMEM((C_HEADS, 1), F32),
                        pltpu.VMEM((C_HEADS, C_KV_LORA), F32)],
    )
    return pl.pallas_call(
        functools.partial(_c_decode_body, scale=scale),
        grid_spec=gs,
        out_shape=jax.ShapeDtypeStruct((nb, C_HEADS, C_KV_LORA), BF16),
        compiler_params=_cp("parallel", "arbitrary"),
        name="c_decode",
    )(page_flat, *([lat_cache] * C_PG), *([rope_cache_t] * C_PG), ql3, qr3, latn3, krn3)


def _rope_tables(pos):
    inv = ROPE_THETA ** (-jnp.arange(0, C_ROPE, 2, dtype=F32) / C_ROPE)
    ang = pos.astype(F32)[:, None] * inv[None, :]
    cos, sin = jnp.cos(ang), jnp.sin(ang)
    n = pos.shape[0]
    ones = jnp.ones((n, C_NOPE), F32)
    zeros = jnp.zeros((n, C_NOPE), F32)
    tail = jnp.zeros((n, C_QW - C_NOPE - C_ROPE), F32)
    ct = jnp.concatenate([ones, cos, cos, tail], axis=1)
    st = jnp.concatenate([zeros, -sin, sin, tail], axis=1)
    return ct, st


def _rope_swap(w):
    half = C_ROPE // 2
    return jnp.concatenate([w[..., half:], w[..., :half]], axis=-1)


def _pad_cols(w, left, total):
    return jnp.pad(w, ((0, 0), (left, total - left - w.shape[1])))


def _mixer_c(xp, xs, bsz, seq, lat_cache, rope_cache, page_table, g,
             c_w_down, c_g_q, c_g_kv, c_w_uq, c_w_uk, c_w_uv, c_w_out):
    nb = xs.shape[0]
    n_pages = page_table.shape[1]
    past = n_pages * PAGE
    w_rope = c_w_down[:, C_DOWN:]
    w_down = jnp.concatenate([c_w_down[:, :C_DOWN], _pad_cols(w_rope, C_NOPE, C_QW),
                              _pad_cols(_rope_swap(w_rope), C_NOPE, C_QW)], axis=1).astype(BF16)
    uq = c_w_uq.reshape(C_Q_LORA, C_HEADS, C_NOPE + C_ROPE)
    zq = jnp.zeros((C_Q_LORA, C_HEADS, C_QW - C_NOPE - C_ROPE), F32)
    wqa = jnp.concatenate([uq, zq], axis=2).reshape(C_Q_LORA, C_HEADS * C_QW).astype(BF16)
    wqb = jnp.concatenate([jnp.zeros((C_Q_LORA, C_HEADS, C_NOPE), F32), _rope_swap(uq[..., C_NOPE:]), zq],
                          axis=2).reshape(C_Q_LORA, C_HEADS * C_QW).astype(BF16)
    wuk = jnp.concatenate([c_w_uk, jnp.zeros((C_KV_LORA, C_HEADS, C_QW - C_NOPE), F32)],
                          axis=2).reshape(C_KV_LORA, C_HEADS * C_QW).astype(BF16)
    wuv = c_w_uv.reshape(C_KV_LORA, C_HEADS * C_V).astype(BF16)
    w_out = c_w_out.astype(BF16)
    gq = c_g_q.reshape(1, -1)
    gkv = c_g_kv.reshape(1, -1)

    h_p = _mm(xp, w_down, g=g)
    ct_p, st_p = _rope_tables(jnp.arange(seq, dtype=I32))
    q, k, v, lat_p, kr_p = _c_prep(h_p, ct_p, st_p, gq, gkv, wqa, wqb, wuk, wuv)
    o = _c_flash(q.reshape(bsz, seq, -1), k.reshape(bsz, seq, -1), v.reshape(bsz, seq, -1))
    xp = _mm(o.reshape(bsz * seq, -1), w_out, res=xp)

    h_s = _mm(xs, w_down, g=g)
    ct_s, st_s = _rope_tables(jnp.full((nb,), past, I32))
    q_s, _, _, lat_s, kr_s = _c_prep(h_s, ct_s, st_s, gq, gkv, wqa, wqb, wuk, wuv)
    wukt = jnp.concatenate([jnp.transpose(c_w_uk, (1, 2, 0)),
                            jnp.zeros((C_HEADS, C_QW - C_NOPE, C_KV_LORA), F32)], axis=1).astype(BF16)
    ql = _head_mm(q_s, wukt, out_dtype=BF16).reshape(nb, C_HEADS, C_KV_LORA)
    qr = q_s.reshape(nb, C_HEADS, C_QW)[:, :, C_NOPE:C_NOPE + C_ROPE]
    o_lat = _c_decode(page_table.reshape(-1), n_pages, lat_cache, jnp.transpose(rope_cache, (0, 2, 1)),
                      ql, qr, lat_s.reshape(nb, 1, -1), kr_s.reshape(nb, 1, -1))
    wuv2 = jnp.transpose(c_w_uv, (1, 0, 2)).reshape(C_HEADS // 2, 2, C_KV_LORA, C_V)
    z = jnp.zeros_like(wuv2[:, 0])
    wuv_pair = jnp.concatenate([jnp.concatenate([wuv2[:, 0], z], axis=2),
                                jnp.concatenate([z, wuv2[:, 1]], axis=2)], axis=1).astype(BF16)
    o_s = _head_mm(o_lat.reshape(nb, C_HEADS * C_KV_LORA), wuv_pair, out_dtype=BF16)
    xs = _mm(o_s, w_out, res=xs)
    return (xp, xs, lat_p.reshape(bsz, seq, -1), kr_p.reshape(bsz, seq, -1),
            lat_s.reshape(nb, 1, -1), kr_s.reshape(nb, 1, -1))


D_Q0 = 0
D_KV0 = D_HEADS * D_HD
D_QI0 = D_KV0 + 2 * D_KV_HEADS * D_HD
D_KI0 = D_QI0 + D_IDX_HEADS * D_IDX_DIM
D_WI0 = D_KI0 + D_IDX_DIM
D_HW = D_WI0 + LANES - D_IDX_DIM
INT_MIN = -2 ** 31


def _sort_key(score):
    bits = pltpu.bitcast(score + 0.0, I32)
    return jnp.where(bits >= 0, bits, bits ^ jnp.int32(0x7FFFFFFF))


def _kth_largest(key_ref, n_chunks, k, rows):
    def count(pred_fn):
        def body(c, acc):
            off = pl.multiple_of(c * LANES, LANES)
            return acc + pred_fn(key_ref[:, pl.ds(off, LANES)], off).astype(I32)
        part = lax.fori_loop(0, n_chunks, body, jnp.zeros((rows, LANES), I32))
        return jnp.sum(part, axis=1, keepdims=True)

    nonneg = count(lambda x, off: x >= 0) >= k
    prefix = jnp.where(nonneg, jnp.int32(0), jnp.int32(INT_MIN))

    def bit_body(i, prefix):
        cand = prefix | lax.shift_left(jnp.int32(1), 30 - i)
        return jnp.where(count(lambda x, off: x >= cand) >= k, cand, prefix)

    thr = lax.fori_loop(0, 31, bit_body, prefix)
    n_gt = count(lambda x, off: x > thr)
    n_eq = count(lambda x, off: x == thr)
    need = k - n_gt
    width = n_chunks * LANES

    def tie_search(_):
        def body(i, lo):
            step = lax.shift_right_logical(jnp.int32(1 << 29), i)
            cand = lo + step
            lane = lax.broadcasted_iota(I32, (rows, LANES), 1)
            c = count(lambda x, off: (x == thr) & (lane + off < cand))
            return jnp.where(c < need, cand, lo)
        lo = lax.fori_loop(0, 30, body, jnp.zeros((rows, 1), I32))
        return lo + 1

    any_tie = jnp.max(jnp.where(n_eq > need, 1, 0)) > 0
    bound = lax.cond(any_tie, tie_search, lambda _: jnp.full((rows, 1), 2 ** 30, I32), 0)
    return thr, bound


def _d_bias_tile(g_ref, h, base):
    t = Q_TILE
    w = g_ref[pl.ds(h, 1), pl.ds(base, 2 * t)]
    x = jnp.broadcast_to(w, (t, 2 * t))
    return pltpu.roll(x, t + 1, 1, stride=1, stride_axis=0)[:, :t]


def _d_prompt_body(q_ref, qi_ref, wi_ref, kv_ref, ki_ref, g_ref, o_ref,
                   key_ref, m_ref, l_ref, acc_ref, *, seq, k_top):
    t = Q_TILE
    blk = pl.program_id(1)
    n_chunks = blk + 1
    row = lax.broadcasted_iota(I32, (t, t), 0)
    col = lax.broadcasted_iota(I32, (t, t), 1)
    qi = qi_ref[0].astype(BF16)
    wi = wi_ref[0][:, D_IDX_DIM:D_IDX_DIM + D_IDX_HEADS] * (D_IDX_HEADS ** -0.5)

    def score_chunk(c, carry):
        off = pl.multiple_of(c * t, t)
        ki = ki_ref[0, pl.ds(off, t), :][:, :D_IDX_DIM].astype(BF16)
        sc = jnp.zeros((t, t), F32)
        for h in range(D_IDX_HEADS):
            dots = _nt(qi[:, h * D_IDX_DIM:(h + 1) * D_IDX_DIM], ki) * (D_IDX_DIM ** -0.5)
            sc = sc + wi[:, h:h + 1] * jnp.maximum(dots, 0.0)
        key = _sort_key(sc)
        key = jnp.where((c < blk) | (col <= row), key, jnp.int32(INT_MIN))
        key_ref[:, pl.ds(off, t)] = key
        return carry

    lax.fori_loop(0, n_chunks, score_chunk, 0)
    thr, bound = _kth_largest(key_ref, n_chunks, k_top, t)

    m_ref[...] = jnp.full(m_ref.shape, NEG, F32)
    l_ref[...] = jnp.zeros(l_ref.shape, F32)
    acc_ref[...] = jnp.zeros(acc_ref.shape, F32)
    q = q_ref[0].astype(BF16)
    rep = D_HEADS // D_KV_HEADS
    kvw = D_KV_HEADS * D_HD

    def attend_chunk(c, carry):
        off = pl.multiple_of(c * t, t)
        key = key_ref[:, pl.ds(off, t)]
        sel = (key > thr) | ((key == thr) & (col + off < bound))
        sel = sel & ((c < blk) | (col <= row))
        kv = kv_ref[0, pl.ds(off, t), :].astype(BF16)
        base = pl.multiple_of(seq - t - (blk - c) * t, t)
        for h in range(D_HEADS):
            gk = h // rep
            s = _nt(q[:, h * D_HD:(h + 1) * D_HD], kv[:, gk * D_HD:(gk + 1) * D_HD]) * (D_HD ** -0.5)
            s = s + _d_bias_tile(g_ref, h, base)
            s = jnp.where(sel, s, NEG)
            m_old = m_ref[h]
            m_new = jnp.maximum(m_old, jnp.max(s, axis=-1, keepdims=True))
            alpha = jnp.exp(m_old - m_new)
            p = jnp.where(sel, jnp.exp(s - m_new), 0.0)
            l_ref[h] = alpha * l_ref[h] + jnp.sum(p, axis=-1, keepdims=True)
            m_ref[h] = m_new
            acc_ref[h] = alpha * acc_ref[h] + _nn(p.astype(BF16), kv[:, kvw + gk * D_HD:kvw + (gk + 1) * D_HD])
        return carry

    lax.fori_loop(0, n_chunks, attend_chunk, 0)
    o_ref[0] = jnp.concatenate([acc_ref[h] / l_ref[h] for h in range(D_HEADS)], axis=-1).astype(o_ref.dtype)


def _d_prompt(h3, gtab, k_top):
    bsz, seq, _ = h3.shape
    t = Q_TILE
    body = functools.partial(_d_prompt_body, seq=seq, k_top=k_top)
    qw = D_HEADS * D_HD
    return pl.pallas_call(
        body,
        grid=(bsz, seq // t),
        in_specs=[pl.BlockSpec((1, t, qw), lambda b, i: (b, i, 0)),
                  pl.BlockSpec((1, t, D_KI0 - D_QI0), lambda b, i: (b, i, D_QI0 // (D_KI0 - D_QI0))),
                  pl.BlockSpec((1, t, LANES), lambda b, i: (b, i, D_KI0 // LANES)),
                  pl.BlockSpec((1, seq, D_QI0 - D_KV0), lambda b, i: (b, 0, D_KV0 // (D_QI0 - D_KV0))),
                  pl.BlockSpec((1, seq, LANES), lambda b, i: (b, 0, D_KI0 // LANES)),
                  pl.BlockSpec(gtab.shape, lambda b, i: (0, 0))],
        out_specs=pl.BlockSpec((1, t, qw), lambda b, i: (b, i, 0)),
        out_shape=jax.ShapeDtypeStruct((bsz, seq, qw), BF16),
        scratch_shapes=[pltpu.VMEM((t, seq), I32),
                        pltpu.VMEM((D_HEADS, t, 1), F32), pltpu.VMEM((D_HEADS, t, 1), F32),
                        pltpu.VMEM((D_HEADS, t, D_HD), F32)],
        compiler_params=_cp("parallel", "arbitrary"),
        name="d_prompt",
    )(h3, h3, h3, h3, h3, gtab)


D_PG = 8


def _d_index_body(pt_ref, *refs):
    page_refs = refs[:D_PG]
    qi_ref, wi_ref, kin_ref, sc_ref, self_ref = refs[D_PG:]
    j = pl.program_id(1)
    qi = qi_ref[0]
    wi = wi_ref[0] * (D_IDX_HEADS ** -0.5)
    qb = qi.astype(BF16)
    parts = []
    for r in page_refs:
        dots = _nn(qb, r[0].astype(BF16)) * (D_IDX_DIM ** -0.5)
        parts.append(jnp.sum(wi * jnp.maximum(dots, 0.0), axis=0, keepdims=True))
    sc_ref[0] = jnp.concatenate(parts, axis=1)

    @pl.when(j == pl.num_programs(1) - 1)
    def _():
        kin = kin_ref[0].astype(BF16).astype(F32)
        dots = jnp.sum(qb.astype(F32) * kin, axis=1, keepdims=True) * (D_IDX_DIM ** -0.5)
        own = jnp.sum(wi * jnp.maximum(dots, 0.0), axis=0, keepdims=True)
        lane = lax.broadcasted_iota(I32, (1, LANES), 1)
        self_ref[0] = jnp.where(lane == 0, own, -jnp.inf)


def _d_index(page_flat, n_pages, kidx_t, qi3, wi3, kin3):
    nb = qi3.shape[0]
    steps = n_pages // D_PG

    def page_spec(i):
        return pl.BlockSpec((1, D_IDX_DIM, PAGE), lambda b, j, pt: (pt[b * n_pages + j * D_PG + i], 0, 0))

    per = lambda shp: pl.BlockSpec((1,) + shp, lambda b, j, pt: (b, 0, 0))
    gs = pltpu.PrefetchScalarGridSpec(
        num_scalar_prefetch=1,
        grid=(nb, steps),
        in_specs=[page_spec(i) for i in range(D_PG)]
        + [per((D_IDX_HEADS, D_IDX_DIM)), per((D_IDX_HEADS, 1)), per((1, D_IDX_DIM))],
        out_specs=[pl.BlockSpec((1, 1, D_PG * PAGE), lambda b, j, pt: (b, 0, j)),
                   per((1, LANES))],
    )
    return pl.pallas_call(
        _d_index_body,
        grid_spec=gs,
        out_shape=[jax.ShapeDtypeStruct((nb, 1, n_pages * PAGE), F32),
                   jax.ShapeDtypeStruct((nb, 1, LANES), F32)],
        compiler_params=_cp("parallel", "arbitrary"),
        name="d_index",
    )(page_flat, *([kidx_t] * D_PG), qi3, wi3, kin3)


def _d_thr_body(sc_ref, thr_ref, bound_ref, key_ref, *, k_top):
    rows, width = sc_ref.shape
    n_chunks = width // LANES

    def fill(c, carry):
        off = pl.multiple_of(c * LANES, LANES)
        key_ref[:, pl.ds(off, LANES)] = _sort_key(sc_ref[:, pl.ds(off, LANES)])
        return carry

    lax.fori_loop(0, n_chunks, fill, 0)
    thr, bound = _kth_largest(key_ref, n_chunks, k_top, rows)
    thr_ref[...] = jnp.broadcast_to(thr, thr_ref.shape)
    bound_ref[...] = jnp.broadcast_to(bound, bound_ref.shape)


def _d_thr(scores, k_top):
    rows, width = scores.shape
    return pl.pallas_call(
        functools.partial(_d_thr_body, k_top=k_top),
        grid=(1,),
        in_specs=[pl.BlockSpec((rows, width), lambda i: (0, 0))],
        out_specs=[pl.BlockSpec((rows, LANES), lambda i: (0, 0))] * 2,
        out_shape=[jax.ShapeDtypeStruct((rows, LANES), I32)] * 2,
        scratch_shapes=[pltpu.VMEM((rows, width), I32)],
        compiler_params=_cp("arbitrary"),
        name="d_thr",
    )(scores)


def _d_decode_body(pt_ref, thr_ref, bnd_ref, *refs):
    kv_refs = refs[:D_PG]
    sc_ref, self_ref, q_ref, kvn_ref, g_ref, o_ref, m_ref, l_ref, acc_ref = refs[D_PG:]
    b = pl.program_id(0)
    j = pl.program_id(1)
    thr = thr_ref[b]
    bound = bnd_ref[b]
    kw = D_KV_HEADS * D_HD
    rep = D_HEADS // D_KV_HEADS
    hrow = lax.broadcasted_iota(I32, (D_HEADS, kw), 0)
    hcol = lax.broadcasted_iota(I32, (D_HEADS, kw), 1)
    diag = (hcol // D_HD) == (hrow // rep)

    @pl.when(j == 0)
    def _():
        m_ref[...] = jnp.full(m_ref.shape, NEG, F32)
        l_ref[...] = jnp.zeros(l_ref.shape, F32)
        acc_ref[...] = jnp.zeros(acc_ref.shape, F32)

    q = q_ref[0]
    qbd = jnp.where(diag, jnp.concatenate([q] * D_KV_HEADS, axis=1), 0.0).astype(BF16)
    kt = jnp.concatenate([r[0, 0] for r in kv_refs], axis=1).astype(BF16)
    vt = jnp.concatenate([r[0, 1] for r in kv_refs], axis=1).astype(BF16)
    width = D_PG * PAGE
    off = pl.multiple_of(j * width, width)
    key = _sort_key(sc_ref[0])
    pos = off + lax.broadcasted_iota(I32, (1, width), 1)
    sel = (key > thr) | ((key == thr) & (pos < bound))
    s = _nn(qbd, kt) * (D_HD ** -0.5) + g_ref[:, pl.ds(off, width)]
    s = jnp.where(sel, s, NEG)
    m_old = m_ref[...]
    m_new = jnp.maximum(m_old, jnp.max(s, axis=-1, keepdims=True))
    alpha = jnp.exp(m_old - m_new)
    p = jnp.where(sel, jnp.exp(s - m_new), 0.0)
    l_ref[...] = alpha * l_ref[...] + jnp.sum(p, axis=-1, keepdims=True)
    acc_ref[...] = alpha * acc_ref[...] + _nt(p.astype(BF16), vt)
    m_ref[...] = m_new

    @pl.when(j == pl.num_programs(1) - 1)
    def _():
        past = pl.num_programs(1) * width
        kn = kvn_ref[0, :, :kw].astype(BF16).astype(F32)
        vn = kvn_ref[0, :, kw:].astype(BF16).astype(F32)
        qf = qbd.astype(F32)
        s_self = (jnp.sum(qf * kn, axis=1, keepdims=True) * (D_HD ** -0.5)
                  + g_ref[:, pl.ds(pl.multiple_of(past, LANES), LANES)][:, 0:1])
        key_s = _sort_key(self_ref[0][:, 0:1])
        sel_s = (key_s > thr) | ((key_s == thr) & (past < bound))
        s_self = jnp.where(sel_s, s_self, NEG)
        m_o = m_ref[...]
        m_n = jnp.maximum(m_o, s_self)
        a = jnp.exp(m_o - m_n)
        p_s = jnp.where(sel_s, jnp.exp(s_self - m_n), 0.0)
        l = a * l_ref[...] + p_s
        acc = a * acc_ref[...] + p_s.astype(BF16).astype(F32) * vn
        o_full = jnp.where(diag, acc / l, 0.0)
        o_ref[0] = (o_full[:, 0:D_HD] + o_full[:, D_HD:2 * D_HD]
                    + o_full[:, 2 * D_HD:3 * D_HD] + o_full[:, 3 * D_HD:]).astype(o_ref.dtype)


def _d_decode(page_flat, thr, bound, n_pages, kv_t, scores3, self3, q3, kvn3, gtab):
    nb = q3.shape[0]
    steps = n_pages // D_PG
    kw = D_KV_HEADS * D_HD

    def kv_spec(i):
        return pl.BlockSpec((1, 2, kw, PAGE),
                            lambda b, j, pt, th, bd: (pt[b * n_pages + j * D_PG + i], 0, 0, 0))

    per = lambda shp: pl.BlockSpec((1,) + shp, lambda b, j, pt, th, bd: (b, 0, 0))
    gs = pltpu.PrefetchScalarGridSpec(
        num_scalar_prefetch=3,
        grid=(nb, steps),
        in_specs=[kv_spec(i) for i in range(D_PG)]
        + [pl.BlockSpec((1, 1, D_PG * PAGE), lambda b, j, pt, th, bd: (b, 0, j)),
           per((1, LANES)), per((D_HEADS, D_HD)), per((1, 2 * kw)),
           pl.BlockSpec(gtab.shape, lambda b, j, pt, th, bd: (0, 0))],
        out_specs=per((D_HEADS, D_HD)),
        scratch_shapes=[pltpu.VMEM((D_HEADS, 1), F32), pltpu.VMEM((D_HEADS, 1), F32),
                        pltpu.VMEM((D_HEADS, kw), F32)],
    )
    return pl.pallas_call(
        _d_decode_body,
        grid_spec=gs,
        out_shape=jax.ShapeDtypeStruct((nb, D_HEADS, D_HD), BF16),
        compiler_params=_cp("parallel", "arbitrary"),
        name="d_decode",
    )(page_flat, thr, bound, *([kv_t] * D_PG), scores3, self3, q3, kvn3, gtab)


def _d_bias_by_dist(rel_bias, n):
    return rel_bias[:, :D_HEADS][_rel_bucket(jnp.arange(n, dtype=I32))].astype(F32).T


def _mixer_d(xp, xs, bsz, seq, kv_cache, kidx_cache, page_table, rel_bias, g, d_w_in, d_w_out):
    nb = xs.shape[0]
    n_pages = page_table.shape[1]
    past = n_pages * PAGE
    w_in = jnp.pad(d_w_in, ((0, 0), (0, D_HW - d_w_in.shape[1]))).astype(BF16)
    w_out = d_w_out.astype(BF16)
    kw = D_KV_HEADS * D_HD

    h_p = _mm(xp, w_in, g=g)
    h3 = h_p.reshape(bsz, seq, D_HW)
    vec = _d_bias_by_dist(rel_bias, seq)
    gtab = jnp.concatenate([vec[:, ::-1], jnp.zeros((D_HEADS, Q_TILE), F32)], axis=1)
    o = _d_prompt(h3, gtab, min(D_TOPK_MAX, seq // 4))
    xp = _mm(o.reshape(bsz * seq, -1), w_out, res=xp)
    kv_p = h3[:, :, D_KV0:D_QI0].reshape(bsz, seq, 2, D_KV_HEADS, D_HD)
    kidx_p = h3[:, :, D_KI0:D_WI0]

    h_s = _mm(xs, w_in, g=g)
    page_flat = page_table.reshape(-1)
    qi3 = h_s[:, D_QI0:D_KI0].reshape(nb, D_IDX_HEADS, D_IDX_DIM)
    wi3 = h_s[:, D_WI0:D_WI0 + D_IDX_HEADS].reshape(nb, D_IDX_HEADS, 1)
    kin3 = h_s[:, D_KI0:D_WI0].reshape(nb, 1, D_IDX_DIM)
    sc3, self3 = _d_index(page_flat, n_pages, jnp.transpose(kidx_cache, (0, 2, 1)), qi3, wi3, kin3)
    scores = jnp.concatenate([sc3.reshape(nb, past), self3.reshape(nb, LANES)], axis=1)
    thr, bound = _d_thr(scores, min(D_TOPK_MAX, (past + 1) // 4))
    vec_s = _d_bias_by_dist(rel_bias, past + 1)
    gtab_s = jnp.concatenate([vec_s[:, :0:-1], jnp.broadcast_to(vec_s[:, 0:1], (D_HEADS, LANES))], axis=1)
    kv_t = jnp.transpose(kv_cache, (0, 2, 3, 4, 1)).reshape(kv_cache.shape[0], 2, kw, PAGE)
    o_s = _d_decode(page_flat, thr[:, 0], bound[:, 0], n_pages, kv_t, sc3, self3,
                    h_s[:, :D_KV0].reshape(nb, D_HEADS, D_HD), h_s[:, D_KV0:D_QI0].reshape(nb, 1, 2 * kw), gtab_s)
    xs = _mm(o_s.reshape(nb, -1), w_out, res=xs)
    kv_s = h_s[:, D_KV0:D_QI0].reshape(nb, 1, 2, D_KV_HEADS, D_HD)
    kidx_s = h_s[:, D_KI0:D_WI0].reshape(nb, 1, D_IDX_DIM)
    return xp, xs, kv_p, kidx_p, kv_s, kidx_s


def kernel(x_prompt, x_sample, mem_prompt, cache_a1_kv, cache_a2_kv, cache_a3_kv, state_b_conv,
           cache_c_latent, cache_c_krope, cache_d_kv, cache_d_kidx, cache_mem_kv, page_table,
           rel_bias, g_mix, g_cross, g_ffn, g_final, w_xq, w_xkv, w_xo, w_ffn_in, w_ffn_out,
           a_w_in, a_w_out, b_w_pw1, b_b_pw1, b_w_dw, b_b_dw, b_ln_g, b_ln_b, b_w_pw2, b_b_pw2,
           c_w_down, c_g_q, c_g_kv, c_w_uq, c_w_uk, c_w_uv, c_w_out, d_w_in, d_w_out):
    bsz, seq, d = x_prompt.shape
    nb = x_sample.shape[0]
    assert x_sample.shape[1] == 1
    depth = g_mix.shape[0]
    xp = x_prompt.reshape(bsz * seq, d)
    xs = x_sample.reshape(nb, d)
    mem2 = mem_prompt.reshape(bsz * MEM_LEN, d)
    hw = X_HEADS * X_HD
    mem_kv_out = []
    outs = {}
    for i in range(depth):
        kind = i % 4
        if kind == 0:
            xp, xs, a_p, a_s = _mixer_a(xp, xs, bsz, seq, [cache_a1_kv, cache_a2_kv, cache_a3_kv],
                                        rel_bias, g_mix[i], a_w_in, a_w_out)
            outs["a_p"], outs["a_s"] = a_p, a_s
        elif kind == 1:
            xp, xs, conv_p, conv_s = _mixer_b(xp, xs, bsz, seq, state_b_conv, g_mix[i], b_w_pw1, b_b_pw1,
                                              b_w_dw, b_b_dw, b_ln_g, b_ln_b, b_w_pw2, b_b_pw2)
            outs["conv"] = (conv_p, conv_s)
        elif kind == 2:
            xp, xs, lat_p, kr_p, lat_s, kr_s = _mixer_c(xp, xs, bsz, seq, cache_c_latent, cache_c_krope,
                                                        page_table, g_mix[i], c_w_down, c_g_q, c_g_kv,
                                                        c_w_uq, c_w_uk, c_w_uv, c_w_out)
            outs["c"] = (lat_p, kr_p, lat_s, kr_s)
        else:
            xp, xs, kv_p, kidx_p, kv_s, kidx_s = _mixer_d(xp, xs, bsz, seq, cache_d_kv, cache_d_kidx,
                                                          page_table, rel_bias, g_mix[i], d_w_in, d_w_out)
            outs["d"] = (kv_p, kidx_p, kv_s, kidx_s)
        mkv = _mm(mem2, w_xkv[i].astype(BF16))
        mem_kv_out.append(mkv.reshape(bsz, MEM_LEN, 2, X_HEADS, X_HD))
        wq = w_xq[i].astype(BF16)
        wo = w_xo[i].astype(BF16)
        xp = _cross(xp.reshape(bsz, seq, d), g_cross[i], wq, wo, mkv.reshape(bsz, MEM_LEN, 2 * hw)).reshape(bsz * seq, d)
        xs = _cross(xs.reshape(nb, 1, d), g_cross[i], wq, wo, cache_mem_kv[i].reshape(nb, MEM_LEN, 2 * hw)).reshape(nb, d)
        w_in = w_ffn_in[i].astype(BF16)
        w_out = w_ffn_out[i].astype(BF16)
        xp = _ffn(xp, g_ffn[i], w_in, w_out)
        xs = _ffn(xs, g_ffn[i], w_in, w_out)
    y_p = _rmsnorm(xp, g_final).reshape(bsz, seq, d)
    y_s = _rmsnorm(xs, g_final).reshape(nb, 1, d)
    a_p, a_s = outs["a_p"], outs["a_s"]
    conv_p, conv_s = outs["conv"]
    lat_p, kr_p, lat_s, kr_s = outs["c"]
    kv_p, kidx_p, kv_s, kidx_s = outs["d"]
    return (y_p, y_s, a_p[0], a_p[1], a_p[2], a_s[0], a_s[1], a_s[2], conv_p, conv_s,
            lat_p, kr_p, lat_s, kr_s, kv_p, kidx_p, kv_s, kidx_s, jnp.stack(mem_kv_out))
```

```python
import functools
import math

import jax
import jax.numpy as jnp
import numpy as np
from jax import lax
from jax.experimental import pallas as pl
from jax.experimental.pallas import tpu as pltpu

F32 = jnp.float32
BF16 = jnp.bfloat16
I32 = jnp.int32

EPS = 1e-6
PAGE = 128
LANES = 128
VMEM_LIMIT = 52 * 1024 * 1024
NEG = -1e30

REL_BUCKETS = 32
REL_MAX_DIST = 2048
A_WINDOWS = (128, 512, 2048)
A_DILATIONS = (1, 4, 16)
A_HEADS = 8
A_HD = 64
A_NKEYS = 129
B_CONV_WIDTH = 31
C_HEADS = 16
C_Q_LORA = 384
C_KV_LORA = 256
C_NOPE = 64
C_ROPE = 32
C_V = 64
ROPE_THETA = 10000.0
D_HEADS = 16
D_KV_HEADS = 4
D_HD = 64
D_IDX_HEADS = 8
D_IDX_DIM = 64
D_TOPK_MAX = 256
X_HEADS = 4
X_HD = 128
MEM_LEN = 256
Q_TILE = 128


def _cp(*sem):
    return pltpu.CompilerParams(dimension_semantics=sem, vmem_limit_bytes=VMEM_LIMIT)


def _nt(a, b):
    return lax.dot_general(a, b, (((1,), (1,)), ((), ())), preferred_element_type=F32)


def _nn(a, b):
    return jnp.dot(a, b, preferred_element_type=F32)


def _sigmoid(x):
    return 1.0 / (1.0 + jnp.exp(-x))


def _rms(x, g):
    return x * lax.rsqrt(jnp.mean(x * x, axis=-1, keepdims=True) + EPS) * g


def _rel_bucket(dist):
    n = jnp.maximum(dist, 0)
    max_exact = REL_BUCKETS // 2
    nf = jnp.maximum(n, 1).astype(F32)
    large = max_exact + (jnp.log(nf / max_exact) / math.log(REL_MAX_DIST / max_exact)
                         * (REL_BUCKETS - max_exact)).astype(I32)
    large = jnp.minimum(large, REL_BUCKETS - 1)
    return jnp.where(n < max_exact, n, large)


def _mm_body(*refs, norm, bias, glu, resid):
    it = iter(refs)
    x_ref = next(it)
    w_ref = next(it)
    g_ref = next(it) if norm else None
    b_ref = next(it) if bias else None
    w2_ref = next(it) if glu else None
    b2_ref = next(it) if (glu and bias) else None
    r_ref = next(it) if resid else None
    o_ref = next(it)
    xn_ref = next(it)

    @pl.when(pl.program_id(1) == 0)
    def _():
        x = x_ref[...].astype(F32)
        if norm:
            x = _rms(x, g_ref[...])
        xn_ref[...] = x.astype(BF16)

    xn = xn_ref[...]
    h = _nn(xn, w_ref[...])
    if bias:
        h = h + b_ref[...]
    if glu:
        h2 = _nn(xn, w2_ref[...])
        if bias:
            h2 = h2 + b2_ref[...]
        h = h * _sigmoid(h2)
    if resid:
        h = h + r_ref[...]
    o_ref[...] = h.astype(o_ref.dtype)


def _mm(x, w, *, g=None, b=None, w2=None, b2=None, res=None, out_dtype=F32, tm=512, tn=None):
    n, k = x.shape
    m = w.shape[1]
    tm = min(tm, n)
    tn = tn or m
    assert n % tm == 0 and m % tn == 0
    args = [x, w]
    specs = [pl.BlockSpec((tm, k), lambda i, j: (i, 0)),
             pl.BlockSpec((k, tn), lambda i, j: (0, j))]
    if g is not None:
        args.append(g.reshape(1, k))
        specs.append(pl.BlockSpec((1, k), lambda i, j: (0, 0)))
    if b is not None:
        args.append(b.reshape(1, m))
        specs.append(pl.BlockSpec((1, tn), lambda i, j: (0, j)))
    if w2 is not None:
        args.append(w2)
        specs.append(pl.BlockSpec((k, tn), lambda i, j: (0, j)))
        if b2 is not None:
            args.append(b2.reshape(1, m))
            specs.append(pl.BlockSpec((1, tn), lambda i, j: (0, j)))
    if res is not None:
        args.append(res)
        specs.append(pl.BlockSpec((tm, tn), lambda i, j: (i, j)))
    body = functools.partial(_mm_body, norm=g is not None, bias=b is not None,
                             glu=w2 is not None, resid=res is not None)
    return pl.pallas_call(
        body,
        grid=(n // tm, m // tn),
        in_specs=specs,
        out_specs=pl.BlockSpec((tm, tn), lambda i, j: (i, j)),
        out_shape=jax.ShapeDtypeStruct((n, m), out_dtype),
        scratch_shapes=[pltpu.VMEM((tm, k), BF16)],
        compiler_params=_cp("parallel", "arbitrary"),
        name="mm",
    )(*args)


def _ffn_body(x_ref, g_ref, wg_ref, wu_ref, wo_ref, o_ref, xn_ref):
    j = pl.program_id(1)

    @pl.when(j == 0)
    def _():
        x = x_ref[...]
        xn_ref[...] = _rms(x, g_ref[...]).astype(BF16)
        o_ref[...] = x

    xn = xn_ref[...]
    hg = _nn(xn, wg_ref[...])
    hu = _nn(xn, wu_ref[...])
    a = (hg * _sigmoid(hg) * hu).astype(BF16)
    o_ref[...] += _nn(a, wo_ref[...])


def _ffn(x, g, w_in, w_out, *, tm=512):
    n, d = x.shape
    dff = w_out.shape[0]
    tf = dff // 2
    assert tf % LANES == 0
    tm = min(tm, n)
    nf = dff // tf
    return pl.pallas_call(
        _ffn_body,
        grid=(n // tm, nf),
        in_specs=[pl.BlockSpec((tm, d), lambda i, j: (i, 0)),
                  pl.BlockSpec((1, d), lambda i, j: (0, 0)),
                  pl.BlockSpec((d, tf), lambda i, j: (0, j)),
                  pl.BlockSpec((d, tf), lambda i, j: (0, j + nf)),
                  pl.BlockSpec((tf, d), lambda i, j: (j, 0))],
        out_specs=pl.BlockSpec((tm, d), lambda i, j: (i, 0)),
        out_shape=jax.ShapeDtypeStruct((n, d), F32),
        scratch_shapes=[pltpu.VMEM((tm, d), BF16)],
        compiler_params=_cp("parallel", "arbitrary"),
        name="ffn",
    )(x, g.reshape(1, d), w_in, w_in, w_out)


def _cross_body(x_ref, g_ref, wq_ref, wo_ref, kv_ref, o_ref):
    x = x_ref[0]
    xn = _rms(x, g_ref[...]).astype(BF16)
    q = _nn(xn, wq_ref[...]).astype(BF16)
    hw = X_HEADS * X_HD
    outs = []
    for h in range(X_HEADS):
        kh = kv_ref[0, :, h * X_HD:(h + 1) * X_HD].astype(BF16)
        vh = kv_ref[0, :, hw + h * X_HD:hw + (h + 1) * X_HD].astype(BF16)
        s = _nt(q[:, h * X_HD:(h + 1) * X_HD], kh) * (X_HD ** -0.5)
        m = jnp.max(s, axis=-1, keepdims=True)
        p = jnp.exp(s - m)
        l = jnp.sum(p, axis=-1, keepdims=True)
        outs.append(_nn((p / l).astype(BF16), vh))
    o = jnp.concatenate(outs, axis=-1).astype(BF16)
    o_ref[0] = x + _nn(o, wo_ref[...])


def _cross(x3, g, wq, wo, kv3, *, tq=512):
    bsz, t, d = x3.shape
    tq = min(tq, t)
    hw = X_HEADS * X_HD
    return pl.pallas_call(
        _cross_body,
        grid=(bsz, t // tq),
        in_specs=[pl.BlockSpec((1, tq, d), lambda b, i: (b, i, 0)),
                  pl.BlockSpec((1, d), lambda b, i: (0, 0)),
                  pl.BlockSpec((d, hw), lambda b, i: (0, 0)),
                  pl.BlockSpec((hw, d), lambda b, i: (0, 0)),
                  pl.BlockSpec((1, MEM_LEN, 2 * hw), lambda b, i: (b, 0, 0))],
        out_specs=pl.BlockSpec((1, tq, d), lambda b, i: (b, i, 0)),
        out_shape=jax.ShapeDtypeStruct((bsz, t, d), F32),
        compiler_params=_cp("parallel", "parallel"),
        name="cross",
    )(x3, g.reshape(1, d), wq, wo, kv3)


def _cross_s_body(q_ref, kv_ref, o_ref):
    def one(j, carry):
        k = kv_ref[0, j, :, 0:X_HEADS, :]
        v = kv_ref[0, j, :, X_HEADS:2 * X_HEADS, :]
        q = q_ref[j]
        s = jnp.sum(k * q[None], axis=-1, keepdims=True) * (X_HD ** -0.5)
        m = jnp.max(s, axis=0, keepdims=True)
        p = jnp.exp(s - m)
        l = jnp.sum(p, axis=0, keepdims=True)
        o_ref[j] = jnp.sum((p / l) * v, axis=0)
        return carry

    lax.fori_loop(0, q_ref.shape[0], one, 0)


def _cross_s(q3, mem_kv5, layer, *, tb=8):
    nb = q3.shape[0]
    tb = min(tb, nb)
    return pl.pallas_call(
        _cross_s_body,
        grid=(nb // tb,),
        in_specs=[pl.BlockSpec((tb, X_HEADS, X_HD), lambda b: (b, 0, 0)),
                  pl.BlockSpec((1, tb, MEM_LEN, 2 * X_HEADS, X_HD), lambda b: (layer, b, 0, 0, 0))],
        out_specs=pl.BlockSpec((tb, X_HEADS, X_HD), lambda b: (b, 0, 0)),
        out_shape=jax.ShapeDtypeStruct((nb, X_HEADS, X_HD), F32),
        compiler_params=_cp("parallel"),
        name="cross_s",
    )(q3, mem_kv5)


def _rms_body(x_ref, g_ref, o_ref):
    o_ref[...] = _rms(x_ref[...], g_ref[...])


def _rmsnorm(x, g, *, tm=512):
    n, d = x.shape
    tm = min(tm, n)
    return pl.pallas_call(
        _rms_body,
        grid=(n // tm,),
        in_specs=[pl.BlockSpec((tm, d), lambda i: (i, 0)),
                  pl.BlockSpec((1, d), lambda i: (0, 0))],
        out_specs=pl.BlockSpec((tm, d), lambda i: (i, 0)),
        out_shape=jax.ShapeDtypeStruct((n, d), F32),
        compiler_params=_cp("parallel"),
        name="rmsnorm",
    )(x, g.reshape(1, d))


def _a_bias_vec(rel_bias, g):
    dist = A_DILATIONS[g] * jnp.arange(A_NKEYS, dtype=I32)
    return rel_bias[_rel_bucket(dist)][:, g * A_HEADS:(g + 1) * A_HEADS].astype(F32).T


def _a_prompt_body(q_ref, kc_ref, kp_ref, vc_ref, vp_ref, bias_ref, o_ref, lse_ref):
    j = pl.program_id(1)
    q = q_ref[0].astype(BF16)
    k = jnp.concatenate([kp_ref[0], kc_ref[0]], axis=0).astype(BF16)
    v = jnp.concatenate([vp_ref[0], vc_ref[0]], axis=0).astype(BF16)
    t = Q_TILE
    row = lax.broadcasted_iota(I32, (t, 2 * t), 0)
    col = lax.broadcasted_iota(I32, (t, 2 * t), 1)
    back = row + t - col
    ok = (back >= 0) & (back <= t) & ((col >= t) | (j > 0))
    outs, lses = [], []
    for h in range(A_HEADS):
        sl = slice(h * A_HD, (h + 1) * A_HD)
        s = _nt(q[:, sl], k[:, sl]) * (A_HD ** -0.5) + bias_ref[h]
        s = jnp.where(ok, s, NEG)
        m = jnp.max(s, axis=-1, keepdims=True)
        p = jnp.exp(s - m)
        l = jnp.sum(p, axis=-1, keepdims=True)
        outs.append(_nn(p.astype(BF16), v[:, sl]) / l)
        lses.append(jnp.broadcast_to(m + jnp.log(l), (t, A_HD)))
    o_ref[0] = jnp.concatenate(outs, axis=-1)
    lse_ref[0] = jnp.concatenate(lses, axis=-1)


def _a_prompt_group(qkv_g, bias_tile):
    nr, sj, _ = qkv_g.shape
    hw = A_HEADS * A_HD
    t = Q_TILE
    assert sj % t == 0

    def cur(which):
        return pl.BlockSpec((1, t, hw), lambda r, j: (r, j, which))

    def prev(which):
        return pl.BlockSpec((1, t, hw), lambda r, j: (r, jnp.maximum(j - 1, 0), which))

    out_spec = pl.BlockSpec((1, t, hw), lambda r, j: (r, j, 0))
    return pl.pallas_call(
        _a_prompt_body,
        grid=(nr, sj // t),
        in_specs=[cur(0), cur(1), prev(1), cur(2), prev(2),
                  pl.BlockSpec((A_HEADS, t, 2 * t), lambda r, j: (0, 0, 0))],
        out_specs=[out_spec, out_spec],
        out_shape=[jax.ShapeDtypeStruct((nr, sj, hw), F32)] * 2,
        compiler_params=_cp("parallel", "parallel"),
        name="a_prompt",
    )(qkv_g, qkv_g, qkv_g, qkv_g, qkv_g, bias_tile)


def _hankel(w, n_rows, n_cols):
    h, p = w.shape
    assert p >= n_rows + n_cols
    flat = jnp.tile(w, (1, n_rows + 1))[:, :n_rows * (p + 1)]
    return flat.reshape(h, n_rows, p + 1)[:, :, :n_cols]


def _a_bias_tile(vec):
    t = Q_TILE
    nh = vec.shape[0]
    w = jnp.concatenate([jnp.zeros((nh, t - 1), F32), vec, jnp.zeros((nh, t), F32)], axis=1)
    return _hankel(w, t, 2 * t)[:, :, ::-1]


def _a_combine_body(o0, o1, o2, l0, l1, l2, x_ref, w_ref, out_ref):
    a0, a1, a2 = l0[...], l1[...], l2[...]
    m = jnp.maximum(jnp.maximum(a0, a1), a2)
    e0, e1, e2 = jnp.exp(a0 - m), jnp.exp(a1 - m), jnp.exp(a2 - m)
    den = e0 + e1 + e2
    o = (e0 / den) * o0[...] + (e1 / den) * o1[...] + (e2 / den) * o2[...]
    out_ref[...] = x_ref[...] + _nn(o.astype(BF16), w_ref[...])


def _a_combine(os_, ls_, x, w_out, *, tm=512):
    n, d = x.shape
    hw = w_out.shape[0]
    tm = min(tm, n)
    small = pl.BlockSpec((tm, hw), lambda i: (i, 0))
    return pl.pallas_call(
        _a_combine_body,
        grid=(n // tm,),
        in_specs=[small] * 6 + [pl.BlockSpec((tm, d), lambda i: (i, 0)),
                                pl.BlockSpec((hw, d), lambda i: (0, 0))],
        out_specs=pl.BlockSpec((tm, d), lambda i: (i, 0)),
        out_shape=jax.ShapeDtypeStruct((n, d), F32),
        compiler_params=_cp("parallel"),
        name="a_combine",
    )(*os_, *ls_, x, w_out)


def _row_to_col(r):
    n = r.shape[1]
    eye = (lax.broadcasted_iota(I32, (LANES, LANES), 0) == lax.broadcasted_iota(I32, (LANES, LANES), 1))
    cols = []
    for c in range(n // LANES):
        blk = jnp.broadcast_to(r[:, c * LANES:(c + 1) * LANES], (LANES, LANES))
        cols.append(jnp.sum(jnp.where(eye, blk, 0.0), axis=1, keepdims=True))
    return jnp.concatenate(cols, axis=0)


def _col_to_row(c):
    n = c.shape[0]
    eye = (lax.broadcasted_iota(I32, (LANES, LANES), 0) == lax.broadcasted_iota(I32, (LANES, LANES), 1))
    rows = []
    for i in range(n // LANES):
        blk = jnp.broadcast_to(c[i * LANES:(i + 1) * LANES, :], (LANES, LANES))
        rows.append(jnp.sum(jnp.where(eye, blk, 0.0), axis=0, keepdims=True))
    return jnp.concatenate(rows, axis=1)


def _a_sample_body(q_ref, kvn_ref, bias_ref, c_ref, o_ref, lse_ref, cn_ref, *, width):
    hw = A_HEADS * A_HD
    q_col = _row_to_col(q_ref[0])
    kvn_row = kvn_ref[0]
    kvn_col = _row_to_col(kvn_row)
    scale = A_HD ** -0.5
    s_rows, self_rows = [], []
    for h in range(A_HEADS):
        sl = slice(h * A_HD, (h + 1) * A_HD)
        kt = c_ref[0, sl, :]
        s_rows.append(jnp.sum(kt * q_col[sl], axis=0, keepdims=True))
        self_rows.append(jnp.sum(kvn_col[sl] * q_col[sl], axis=0, keepdims=True))
    bias = bias_ref[...]
    s = jnp.concatenate(s_rows, axis=0) * scale + bias[:, :width]
    s_self = jnp.concatenate(self_rows, axis=0) * scale + bias[:, width:width + 1]
    m = jnp.maximum(jnp.max(s, axis=-1, keepdims=True), s_self)
    p = jnp.exp(s - m)
    p_self = jnp.exp(s_self - m)
    l = jnp.sum(p, axis=-1, keepdims=True) + p_self
    lse = m + jnp.log(l)
    o_cols, lse_cols = [], []
    for h in range(A_HEADS):
        sl = slice(hw + h * A_HD, hw + (h + 1) * A_HD)
        vt = c_ref[0, sl, :]
        oc = jnp.sum(vt * p[h:h + 1, :], axis=1, keepdims=True) + kvn_col[sl] * p_self[h:h + 1, :]
        o_cols.append(oc / l[h:h + 1, :])
        lse_cols.append(jnp.broadcast_to(lse[h:h + 1, :], (A_HD, 1)))
    o_ref[0] = _col_to_row(jnp.concatenate(o_cols, axis=0))
    lse_ref[0] = _col_to_row(jnp.concatenate(lse_cols, axis=0))
    rc = 128
    lane = lax.broadcasted_iota(I32, (rc, width), 1)
    for r0 in range(0, 2 * hw, rc):
        rolled = pltpu.roll(c_ref[0, r0:r0 + rc, :], width - 1, 1)
        cn_ref[0, r0:r0 + rc, :] = jnp.where(lane == width - 1, kvn_col[r0:r0 + rc], rolled)


def _a_sample_group(q_g, kvn_g, cache_t, bias_s):
    bsz, rows, width = cache_t.shape
    hw = A_HEADS * A_HD
    body = functools.partial(_a_sample_body, width=width)
    return pl.pallas_call(
        body,
        grid=(bsz,),
        in_specs=[pl.BlockSpec((1, 1, hw), lambda b: (b, 0, 0)),
                  pl.BlockSpec((1, 1, 2 * hw), lambda b: (b, 0, 0)),
                  pl.BlockSpec((A_HEADS, width + LANES), lambda b: (0, 0)),
                  pl.BlockSpec((1, rows, width), lambda b: (b, 0, 0))],
        out_specs=[pl.BlockSpec((1, 1, hw), lambda b: (b, 0, 0)),
                   pl.BlockSpec((1, 1, hw), lambda b: (b, 0, 0)),
                   pl.BlockSpec((1, rows, width), lambda b: (b, 0, 0))],
        out_shape=[jax.ShapeDtypeStruct((bsz, 1, hw), F32),
                   jax.ShapeDtypeStruct((bsz, 1, hw), F32),
                   jax.ShapeDtypeStruct((bsz, rows, width), F32)],
        compiler_params=_cp("parallel"),
        name="a_sample",
    )(q_g, kvn_g, bias_s, cache_t)


def _mixer_a(xp, xs, bsz, seq, caches, rel_bias, g, a_w_in, a_w_out):
    hw = A_HEADS * A_HD
    w_in = a_w_in.astype(BF16)
    w_out = a_w_out.astype(BF16)
    qkv_s = _mm(xs, w_in, g=g, tn=w_in.shape[1] // 2)
    os_p, ls_p, os_s, ls_s, new_p, new_s = [], [], [], [], [], []
    nb = xs.shape[0]
    dm = xp.shape[1]
    for gi in range(3):
        vec = _a_bias_vec(rel_bias, gi)
        d = A_DILATIONS[gi]
        sj = seq // d
        x_g = jnp.transpose(xp.reshape(bsz, sj, d, dm), (0, 2, 1, 3)).reshape(bsz * seq, dm) if d > 1 else xp
        w_g = jnp.concatenate([w_in[:, (3 * j + gi) * hw:(3 * j + gi + 1) * hw] for j in range(3)], axis=1)
        qkv_g = _mm(x_g, w_g, g=g).reshape(bsz * d, sj, 3 * hw)
        o, lse = _a_prompt_group(qkv_g, _a_bias_tile(vec))

        def natural(a):
            return jnp.transpose(a.reshape(bsz, d, sj, -1), (0, 2, 1, 3)).reshape(bsz * seq, -1)

        os_p.append(natural(o))
        ls_p.append(natural(lse))
        w = min(A_WINDOWS[gi], seq)
        assert w % d == 0
        kv_tail = qkv_g.reshape(bsz, d, sj, 3 * hw)[:, :, sj - w // d:, hw:]
        new_p.append(jnp.transpose(kv_tail, (0, 2, 1, 3)).reshape(bsz, w, 2, A_HEADS, A_HD))
        cache = caches[gi]
        width = cache.shape[1]
        assert width == (A_NKEYS - 1) * d
        cache_t = jnp.transpose(cache, (0, 2, 3, 4, 1)).reshape(nb, 2 * hw, width)
        q_s = qkv_s[:, gi * hw:(gi + 1) * hw].reshape(nb, 1, hw)
        kvn = jnp.concatenate([qkv_s[:, (3 + gi) * hw:(4 + gi) * hw],
                               qkv_s[:, (6 + gi) * hw:(7 + gi) * hw]], axis=1).reshape(nb, 1, 2 * hw)
        lane = jnp.arange(width)
        kk = (width - lane) // d
        bias_c = jnp.where((lane % d == 0)[None, :], vec[:, jnp.clip(kk, 0, A_NKEYS - 1)], NEG)
        bias_s = jnp.concatenate([bias_c, jnp.broadcast_to(vec[:, 0:1], (A_HEADS, LANES))], axis=1)
        o_s, lse_s, cache_new = _a_sample_group(q_s, kvn, cache_t, bias_s)
        os_s.append(o_s.reshape(nb, hw))
        ls_s.append(lse_s.reshape(nb, hw))
        new_s.append(jnp.transpose(cache_new.reshape(nb, 2, A_HEADS, A_HD, width), (0, 4, 1, 2, 3)))
    xp = _a_combine(os_p, ls_p, xp, w_out)
    xs = _a_combine(os_s, ls_s, xs, w_out)
    return xp, xs, new_p, new_s


def _ln_silu_proj(c, lng_ref, lnb_ref, w2_ref, b2_ref, x):
    mu = jnp.mean(c, axis=-1, keepdims=True)
    xc = c - mu
    y = xc * lax.rsqrt(jnp.mean(xc * xc, axis=-1, keepdims=True) + EPS)
    y = y * lng_ref[...] + lnb_ref[...]
    y = y * _sigmoid(y)
    return x + _nn(y.astype(BF16), w2_ref[...]) + b2_ref[...]


def _b_prompt_body(uc_ref, up_ref, x_ref, wdw_ref, bdw_ref, lng_ref, lnb_ref, w2_ref, b2_ref,
                   o_ref, buf_ref, c_ref, *, tq):
    i = pl.program_id(1)
    pad = 32
    buf_ref[0:pad, :] = jnp.where(i > 0, up_ref[0], 0.0)
    buf_ref[pad:pad + tq, :] = uc_ref[0]
    off = pad - (B_CONV_WIDTH - 1)
    rc, cc = 128, 256
    d = uc_ref.shape[2]
    for r0 in range(0, tq, rc):
        for c0 in range(0, d, cc):
            acc = jnp.zeros((rc, cc), F32)
            for w in range(B_CONV_WIDTH):
                acc = acc + buf_ref[r0 + off + w:r0 + off + w + rc, c0:c0 + cc] * wdw_ref[w:w + 1, c0:c0 + cc]
            c_ref[r0:r0 + rc, c0:c0 + cc] = acc
    c = c_ref[...] + bdw_ref[...]
    o_ref[0] = _ln_silu_proj(c, lng_ref, lnb_ref, w2_ref, b2_ref, x_ref[0])


def _b_prompt(u3, x3, wdw, bdw, lng, lnb, w2, b2, *, tq=256):
    bsz, s, d = u3.shape
    tq = min(tq, s)
    pad = 32
    body = functools.partial(_b_prompt_body, tq=tq)
    vec = lambda: pl.BlockSpec((1, d), lambda b, i: (0, 0))
    return pl.pallas_call(
        body,
        grid=(bsz, s // tq),
        in_specs=[pl.BlockSpec((1, tq, d), lambda b, i: (b, i, 0)),
                  pl.BlockSpec((1, pad, d), lambda b, i: (b, jnp.maximum(i * (tq // pad) - 1, 0), 0)),
                  pl.BlockSpec((1, tq, d), lambda b, i: (b, i, 0)),
                  pl.BlockSpec((B_CONV_WIDTH, d), lambda b, i: (0, 0)),
                  vec(), vec(), vec(),
                  pl.BlockSpec((d, d), lambda b, i: (0, 0)),
                  vec()],
        out_specs=pl.BlockSpec((1, tq, d), lambda b, i: (b, i, 0)),
        out_shape=jax.ShapeDtypeStruct((bsz, s, d), F32),
        scratch_shapes=[pltpu.VMEM((tq + pad, d), F32), pltpu.VMEM((tq, d), F32)],
        compiler_params=_cp("parallel", "parallel"),
        name="b_prompt",
    )(u3, u3, x3, wdw, bdw.reshape(1, d), lng.reshape(1, d), lnb.reshape(1, d), w2, b2.reshape(1, d))


def _b_sample_body(st_ref, u_ref, x_ref, wdw_ref, bdw_ref, lng_ref, lnb_ref, w2_ref, b2_ref,
                   o_ref, ns_ref):
    nw = B_CONV_WIDTH - 1
    u = u_ref[...]
    acc = u * wdw_ref[nw:nw + 1, :] + bdw_ref[...]
    for w in range(nw):
        acc = acc + st_ref[w] * wdw_ref[w:w + 1, :]
        if w > 0:
            ns_ref[w - 1] = st_ref[w]
    ns_ref[nw - 1] = u
    o_ref[...] = _ln_silu_proj(acc, lng_ref, lnb_ref, w2_ref, b2_ref, x_ref[...])


def _b_sample(state_t, u, x, wdw, bdw, lng, lnb, w2, b2, *, tb=32):
    nw, nb, d = state_t.shape
    tb = min(tb, nb)
    vec = lambda: pl.BlockSpec((1, d), lambda i: (0, 0))
    return pl.pallas_call(
        _b_sample_body,
        grid=(nb // tb,),
        in_specs=[pl.BlockSpec((nw, tb, d), lambda i: (0, i, 0)),
                  pl.BlockSpec((tb, d), lambda i: (i, 0)),
                  pl.BlockSpec((tb, d), lambda i: (i, 0)),
                  pl.BlockSpec((B_CONV_WIDTH, d), lambda i: (0, 0)),
                  vec(), vec(), vec(),
                  pl.BlockSpec((d, d), lambda i: (0, 0)),
                  vec()],
        out_specs=[pl.BlockSpec((tb, d), lambda i: (i, 0)),
                   pl.BlockSpec((nw, tb, d), lambda i: (0, i, 0))],
        out_shape=[jax.ShapeDtypeStruct((nb, d), F32),
                   jax.ShapeDtypeStruct((nw, nb, d), F32)],
        compiler_params=_cp("parallel"),
        name="b_sample",
    )(state_t, u, x, wdw, bdw.reshape(1, d), lng.reshape(1, d), lnb.reshape(1, d), w2, b2.reshape(1, d))


def _mixer_b(xp, xs, bsz, seq, state, g, w_pw1, b_pw1, w_dw, b_dw, ln_g, ln_b, w_pw2, b_pw2):
    d = xp.shape[1]
    wa = w_pw1[:, :d].astype(BF16)
    wb = w_pw1[:, d:].astype(BF16)
    w2 = w_pw2.astype(BF16)
    nw = B_CONV_WIDTH - 1
    u_p = _mm(xp, wa, g=g, b=b_pw1[:d], w2=wb, b2=b_pw1[d:])
    u_s = _mm(xs, wa, g=g, b=b_pw1[:d], w2=wb, b2=b_pw1[d:])
    u3 = u_p.reshape(bsz, seq, d)
    xp = _b_prompt(u3, xp.reshape(bsz, seq, d), w_dw, b_dw, ln_g, ln_b, w2, b_pw2).reshape(bsz * seq, d)
    conv_p = u3[:, seq - nw:]
    xs, ns_t = _b_sample(jnp.transpose(state, (1, 0, 2)), u_s, xs, w_dw, b_dw, ln_g, ln_b, w2, b_pw2)
    return xp, xs, conv_p, jnp.transpose(ns_t, (1, 0, 2))


C_QW = 128
C_DOWN = C_Q_LORA + C_KV_LORA


def _c_prep_body(h_ref, ct_ref, st_ref, gq_ref, gkv_ref, wqa_ref, wqb_ref, wuk_ref, wuv_ref,
                 q_ref, k_ref, v_ref, lat_ref, kr_ref):
    h = h_ref[...]
    ct = ct_ref[...]
    st = st_ref[...]
    cq = _rms(h[:, :C_Q_LORA], gq_ref[...]).astype(BF16)
    lat = _rms(h[:, C_Q_LORA:C_DOWN], gkv_ref[...])
    lat_ref[...] = lat
    lat_b = lat.astype(BF16)
    krp = h[:, C_DOWN:C_DOWN + C_QW] * ct + h[:, C_DOWN + C_QW:C_DOWN + 2 * C_QW] * st
    kr_ref[...] = krp[:, C_NOPE:C_NOPE + C_ROPE]
    qa = _nn(cq, wqa_ref[...])
    qb = _nn(cq, wqb_ref[...])
    kn = _nn(lat_b, wuk_ref[...])
    for hd in range(C_HEADS):
        sl = slice(hd * C_QW, (hd + 1) * C_QW)
        q_ref[:, sl] = (qa[:, sl] * ct + qb[:, sl] * st).astype(BF16)
        k_ref[:, sl] = (kn[:, sl] + krp).astype(BF16)
    v_ref[...] = _nn(lat_b, wuv_ref[...]).astype(BF16)


def _c_prep(h, ct, st, gq, gkv, wqa, wqb, wuk, wuv, *, tm=512):
    n, hc = h.shape
    tm = min(tm, n)
    nt = ct.shape[0] // tm
    qw = C_HEADS * C_QW
    vw = C_HEADS * C_V
    full = lambda a: pl.BlockSpec(a.shape, lambda i: (0, 0))
    return pl.pallas_call(
        _c_prep_body,
        grid=(n // tm,),
        in_specs=[pl.BlockSpec((tm, hc), lambda i: (i, 0)),
                  pl.BlockSpec((tm, C_QW), lambda i: (i % nt, 0)),
                  pl.BlockSpec((tm, C_QW), lambda i: (i % nt, 0)),
                  full(gq), full(gkv), full(wqa), full(wqb), full(wuk), full(wuv)],
        out_specs=[pl.BlockSpec((tm, qw), lambda i: (i, 0)),
                   pl.BlockSpec((tm, qw), lambda i: (i, 0)),
                   pl.BlockSpec((tm, vw), lambda i: (i, 0)),
                   pl.BlockSpec((tm, C_KV_LORA), lambda i: (i, 0)),
                   pl.BlockSpec((tm, C_ROPE), lambda i: (i, 0))],
        out_shape=[jax.ShapeDtypeStruct((n, qw), BF16),
                   jax.ShapeDtypeStruct((n, qw), BF16),
                   jax.ShapeDtypeStruct((n, vw), BF16),
                   jax.ShapeDtypeStruct((n, C_KV_LORA), F32),
                   jax.ShapeDtypeStruct((n, C_ROPE), F32)],
        compiler_params=_cp("parallel"),
        name="c_prep",
    )(h, ct, st, gq, gkv, wqa, wqb, wuk, wuv)


def _c_flash_body(q_ref, k_ref, vt_ref, ot_ref, m_ref, l_ref, acc_ref, s_ref, p_ref, a_ref, *, tq, scale):
    i = pl.program_id(2)
    t = Q_TILE
    m_ref[...] = jnp.full(m_ref.shape, NEG, F32)
    l_ref[...] = jnp.zeros(l_ref.shape, F32)
    acc_ref[...] = jnp.zeros(acc_ref.shape, F32)
    q = q_ref[0]
    krow = lax.broadcasted_iota(I32, (t, tq), 0)
    qcol = lax.broadcasted_iota(I32, (t, tq), 1)
    n_full = i * (tq // t)

    def chunk(c, masked):
        off = pl.multiple_of(c * t, t)
        kc = k_ref[0, pl.ds(off, t), :]
        vtc = vt_ref[0, :, pl.ds(off, t)]
        for hh in range(C_GRP):
            s_ref[hh] = _nt(kc[:, hh * C_QW:(hh + 1) * C_QW], q[:, hh * C_QW:(hh + 1) * C_QW])
        for hh in range(C_GRP):
            s = s_ref[hh] * scale
            if masked:
                s = jnp.where(off + krow <= i * tq + qcol, s, NEG)
            s_ref[hh] = s
            m_old = m_ref[hh]
            m_new = jnp.maximum(m_old, jnp.max(s, axis=0, keepdims=True))
            a_ref[hh] = jnp.exp(m_old - m_new)
            m_ref[hh] = m_new
        for hh in range(C_GRP):
            p = jnp.exp(s_ref[hh] - m_ref[hh])
            l_ref[hh] = a_ref[hh] * l_ref[hh] + jnp.sum(p, axis=0, keepdims=True)
            p_ref[hh] = p.astype(BF16)
        for hh in range(C_GRP):
            rows = slice(hh * C_V, (hh + 1) * C_V)
            acc_ref[rows, :] = a_ref[hh] * acc_ref[rows, :] + _nn(vtc[rows, :], p_ref[hh])

    def full_chunk(c, carry):
        chunk(c, False)
        return carry

    lax.fori_loop(0, n_full, full_chunk, 0)
    for dc in range(tq // t):
        chunk(n_full + dc, True)
    for hh in range(C_GRP):
        rows = slice(hh * C_V, (hh + 1) * C_V)
        ot_ref[0, rows, :] = (acc_ref[rows, :] / l_ref[hh]).astype(ot_ref.dtype)


C_GRP = 4


def _c_flash(q3, k3, vt3, *, tq=256):
    bsz, s, _ = q3.shape
    tq = min(tq, s)
    ngrp = C_HEADS // C_GRP
    scale = (C_NOPE + C_ROPE) ** -0.5
    body = functools.partial(_c_flash_body, tq=tq, scale=scale)
    return pl.pallas_call(
        body,
        grid=(bsz, ngrp, s // tq),
        in_specs=[pl.BlockSpec((1, tq, C_GRP * C_QW), lambda b, hp, i: (b, i, hp)),
                  pl.BlockSpec((1, s, C_GRP * C_QW), lambda b, hp, i: (b, 0, hp)),
                  pl.BlockSpec((1, C_GRP * C_V, s), lambda b, hp, i: (b, hp, 0))],
        out_specs=pl.BlockSpec((1, C_GRP * C_V, tq), lambda b, hp, i: (b, hp, i)),
        out_shape=jax.ShapeDtypeStruct((bsz, C_HEADS * C_V, s), BF16),
        scratch_shapes=[pltpu.VMEM((C_GRP, 1, tq), F32), pltpu.VMEM((C_GRP, 1, tq), F32),
                        pltpu.VMEM((C_GRP * C_V, tq), F32),
                        pltpu.VMEM((C_GRP, Q_TILE, tq), F32), pltpu.VMEM((C_GRP, Q_TILE, tq), BF16),
                        pltpu.VMEM((C_GRP, 1, tq), F32)],
        compiler_params=_cp("parallel", "parallel", "arbitrary"),
        name="c_flash",
    )(q3, k3, vt3)


def _head_mm_body(x_ref, w_ref, o_ref):
    o_ref[...] = _nn(x_ref[...], w_ref[0]).astype(o_ref.dtype)


def _head_mm(x, w3, *, out_dtype=F32):
    n = x.shape[0]
    nh, kin, kout = w3.shape
    return pl.pallas_call(
        _head_mm_body,
        grid=(nh,),
        in_specs=[pl.BlockSpec((n, kin), lambda h: (0, h)),
                  pl.BlockSpec((1, kin, kout), lambda h: (h, 0, 0))],
        out_specs=pl.BlockSpec((n, kout), lambda h: (0, h)),
        out_shape=jax.ShapeDtypeStruct((n, nh * kout), out_dtype),
        compiler_params=_cp("parallel"),
        name="head_mm",
    )(x, w3)


C_PG = 16


def _c_decode_body(pt_ref, *refs, scale):
    lat_refs = refs[:C_PG]
    rope_refs = refs[C_PG:2 * C_PG]
    ql_ref, qr_ref, latn_ref, krn_ref, o_ref, m_ref, l_ref, acc_ref = refs[2 * C_PG:]
    j = pl.program_id(1)

    @pl.when(j == 0)
    def _():
        m_ref[...] = jnp.full(m_ref.shape, NEG, F32)
        l_ref[...] = jnp.zeros(l_ref.shape, F32)
        acc_ref[...] = jnp.zeros(acc_ref.shape, F32)

    ql = ql_ref[0]
    qr = qr_ref[0]
    lat = jnp.concatenate([r[0] for r in lat_refs], axis=0).astype(BF16)
    s_rope = jnp.concatenate([_nn(qr, r[0].astype(BF16)) for r in rope_refs], axis=1)
    s = (_nt(ql, lat) + s_rope) * scale
    m_old = m_ref[...]
    m_new = jnp.maximum(m_old, jnp.max(s, axis=-1, keepdims=True))
    alpha = jnp.exp(m_old - m_new)
    p = jnp.exp(s - m_new)
    l_ref[...] = alpha * l_ref[...] + jnp.sum(p, axis=-1, keepdims=True)
    acc_ref[...] = alpha * acc_ref[...] + _nn(p.astype(BF16), lat)
    m_ref[...] = m_new

    @pl.when(j == pl.num_programs(1) - 1)
    def _():
        latn = latn_ref[0].astype(BF16)
        krn = krn_ref[0].astype(BF16)
        s_self = (jnp.sum(ql.astype(F32) * latn.astype(F32), axis=1, keepdims=True)
                  + jnp.sum(qr.astype(F32) * krn.astype(F32), axis=1, keepdims=True)) * scale
        m_o = m_ref[...]
        m_n = jnp.maximum(m_o, s_self)
        a = jnp.exp(m_o - m_n)
        p_s = jnp.exp(s_self - m_n)
        l = a * l_ref[...] + p_s
        acc = a * acc_ref[...] + p_s.astype(BF16).astype(F32) * latn.astype(F32)
        o_ref[0] = (acc / l).astype(o_ref.dtype)


def _c_decode(page_flat, n_pages, lat_cache, rope_cache_t, ql3, qr3, latn3, krn3):
    nb = ql3.shape[0]
    scale = (C_NOPE + C_ROPE) ** -0.5
    steps = n_pages // C_PG

    def lat_spec(i):
        return pl.BlockSpec((1, PAGE, C_KV_LORA), lambda b, j, pt: (pt[b * n_pages + j * C_PG + i], 0, 0))

    def rope_spec(i):
        return pl.BlockSpec((1, C_ROPE, PAGE), lambda b, j, pt: (pt[b * n_pages + j * C_PG + i], 0, 0))

    per = lambda shp: pl.BlockSpec((1,) + shp, lambda b, j, pt: (b, 0, 0))
    gs = pltpu.PrefetchScalarGridSpec(
        num_scalar_prefetch=1,
        grid=(nb, steps),
        in_specs=[lat_spec(i) for i in range(C_PG)] + [rope_spec(i) for i in range(C_PG)]
        + [per((C_HEADS, C_KV_LORA)), per((C_HEADS, C_ROPE)), per((1, C_KV_LORA)), per((1, C_ROPE))],
        out_specs=per((C_HEADS, C_KV_LORA)),
        scratch_shapes=[pltpu.VMEM((C_HEADS, 1), F32), pltpu.VMEM((C_HEADS, 1), F32),
                        pltpu.VMEM((C_HEADS, C_KV_LORA), F32)],
    )
    return pl.pallas_call(
        functools.partial(_c_decode_body, scale=scale),
        grid_spec=gs,
        out_shape=jax.ShapeDtypeStruct((nb, C_HEADS, C_KV_LORA), BF16),
        compiler_params=_cp("parallel", "arbitrary"),
        name="c_decode",
    )(page_flat, *([lat_cache] * C_PG), *([rope_cache_t] * C_PG), ql3, qr3, latn3, krn3)


def _rope_tables(pos):
    inv = ROPE_THETA ** (-jnp.arange(0, C_ROPE, 2, dtype=F32) / C_ROPE)
    ang = pos.astype(F32)[:, None] * inv[None, :]
    cos, sin = jnp.cos(ang), jnp.sin(ang)
    n = pos.shape[0]
    ones = jnp.ones((n, C_NOPE), F32)
    zeros = jnp.zeros((n, C_NOPE), F32)
    tail = jnp.zeros((n, C_QW - C_NOPE - C_ROPE), F32)
    ct = jnp.concatenate([ones, cos, cos, tail], axis=1)
    st = jnp.concatenate([zeros, -sin, sin, tail], axis=1)
    return ct, st


def _rope_swap(w):
    half = C_ROPE // 2
    return jnp.concatenate([w[..., half:], w[..., :half]], axis=-1)


def _pad_cols(w, left, total):
    return jnp.pad(w, ((0, 0), (left, total - left - w.shape[1])))


def _mixer_c(xp, xs, bsz, seq, lat_cache, rope_cache, page_table, g,
             c_w_down, c_g_q, c_g_kv, c_w_uq, c_w_uk, c_w_uv, c_w_out):
    nb = xs.shape[0]
    n_pages = page_table.shape[1]
    past = n_pages * PAGE
    w_rope = c_w_down[:, C_DOWN:]
    w_down = jnp.concatenate([c_w_down[:, :C_DOWN], _pad_cols(w_rope, C_NOPE, C_QW),
                              _pad_cols(_rope_swap(w_rope), C_NOPE, C_QW)], axis=1).astype(BF16)
    uq = c_w_uq.reshape(C_Q_LORA, C_HEADS, C_NOPE + C_ROPE)
    zq = jnp.zeros((C_Q_LORA, C_HEADS, C_QW - C_NOPE - C_ROPE), F32)
    wqa = jnp.concatenate([uq, zq], axis=2).reshape(C_Q_LORA, C_HEADS * C_QW).astype(BF16)
    wqb = jnp.concatenate([jnp.zeros((C_Q_LORA, C_HEADS, C_NOPE), F32), _rope_swap(uq[..., C_NOPE:]), zq],
                          axis=2).reshape(C_Q_LORA, C_HEADS * C_QW).astype(BF16)
    wuk = jnp.concatenate([c_w_uk, jnp.zeros((C_KV_LORA, C_HEADS, C_QW - C_NOPE), F32)],
                          axis=2).reshape(C_KV_LORA, C_HEADS * C_QW).astype(BF16)
    wuv = c_w_uv.reshape(C_KV_LORA, C_HEADS * C_V).astype(BF16)
    w_out = c_w_out.astype(BF16)
    gq = c_g_q.reshape(1, -1)
    gkv = c_g_kv.reshape(1, -1)

    h_p = _mm(xp, w_down, g=g)
    ct_p, st_p = _rope_tables(jnp.arange(seq, dtype=I32))
    q, k, v, lat_p, kr_p = _c_prep(h_p, ct_p, st_p, gq, gkv, wqa, wqb, wuk, wuv)
    vt = jnp.transpose(v.reshape(bsz, seq, -1), (0, 2, 1))
    ot = _c_flash(q.reshape(bsz, seq, -1), k.reshape(bsz, seq, -1), vt)
    xp = _mm(jnp.transpose(ot, (0, 2, 1)).reshape(bsz * seq, -1), w_out, res=xp)

    h_s = _mm(xs, w_down, g=g)
    ct_s, st_s = _rope_tables(jnp.full((nb,), past, I32))
    q_s, _, _, lat_s, kr_s = _c_prep(h_s, ct_s, st_s, gq, gkv, wqa, wqb, wuk, wuv)
    wukt = jnp.concatenate([jnp.transpose(c_w_uk, (1, 2, 0)),
                            jnp.zeros((C_HEADS, C_QW - C_NOPE, C_KV_LORA), F32)], axis=1).astype(BF16)
    ql = _head_mm(q_s, wukt, out_dtype=BF16).reshape(nb, C_HEADS, C_KV_LORA)
    qr = q_s.reshape(nb, C_HEADS, C_QW)[:, :, C_NOPE:C_NOPE + C_ROPE]
    o_lat = _c_decode(page_table.reshape(-1), n_pages, lat_cache, jnp.transpose(rope_cache, (0, 2, 1)),
                      ql, qr, lat_s.reshape(nb, 1, -1), kr_s.reshape(nb, 1, -1))
    wuv2 = jnp.transpose(c_w_uv, (1, 0, 2)).reshape(C_HEADS // 2, 2, C_KV_LORA, C_V)
    z = jnp.zeros_like(wuv2[:, 0])
    wuv_pair = jnp.concatenate([jnp.concatenate([wuv2[:, 0], z], axis=2),
                                jnp.concatenate([z, wuv2[:, 1]], axis=2)], axis=1).astype(BF16)
    o_s = _head_mm(o_lat.reshape(nb, C_HEADS * C_KV_LORA), wuv_pair, out_dtype=BF16)
    xs = _mm(o_s, w_out, res=xs)
    return (xp, xs, lat_p.reshape(bsz, seq, -1), kr_p.reshape(bsz, seq, -1),
            lat_s.reshape(nb, 1, -1), kr_s.reshape(nb, 1, -1))


D_Q0 = 0
D_KV0 = D_HEADS * D_HD
D_QI0 = D_KV0 + 2 * D_KV_HEADS * D_HD
D_KI0 = D_QI0 + D_IDX_HEADS * D_IDX_DIM
D_WI0 = D_KI0 + D_IDX_DIM
D_HW = D_WI0 + LANES - D_IDX_DIM
INT_MIN = -2 ** 31


def _sort_key(score):
    bits = pltpu.bitcast(score, I32)
    return jnp.where(bits >= 0, bits, bits ^ jnp.int32(0x7FFFFFFF))


CNT_UNROLL = 4


def _kth_largest(key_ref, n_groups, k):
    nq = key_ref.shape[1]
    kf = jnp.float32(k)

    def count(pred_fn):
        def body(gi, acc):
            for u in range(CNT_UNROLL):
                off = pl.multiple_of((gi * CNT_UNROLL + u) * LANES, LANES)
                hit = jnp.where(pred_fn(key_ref[pl.ds(off, LANES), :], off), 1.0, 0.0)
                acc = acc + jnp.sum(hit.reshape(LANES // 8, 8, nq), axis=0)
            return acc
        part = lax.fori_loop(0, n_groups, body, jnp.zeros((8, nq), F32))
        return jnp.sum(part, axis=0, keepdims=True)

    nonneg = count(lambda x, off: x >= 0) >= kf
    prefix = jnp.where(nonneg, jnp.int32(0), jnp.int32(INT_MIN))

    def bit_body(i, prefix):
        cand = prefix | lax.shift_left(jnp.int32(1), 30 - i)
        return jnp.where(count(lambda x, off: x >= cand) >= kf, cand, prefix)

    thr = lax.fori_loop(0, 31, bit_body, prefix)
    n_gt = count(lambda x, off: x > thr)
    n_eq = count(lambda x, off: x == thr)
    need = kf - n_gt

    def tie_search(_):
        row = lax.broadcasted_iota(I32, (LANES, nq), 0)

        def body(i, lo):
            cand = lo + lax.shift_right_logical(jnp.int32(1 << 29), i)
            c = count(lambda x, off: (x == thr) & (row + off < cand))
            return jnp.where(c < need, cand, lo)
        return lax.fori_loop(0, 30, body, jnp.zeros((1, nq), I32)) + 1

    any_tie = jnp.max(jnp.where(n_eq > need, 1.0, 0.0)) > 0.0
    bound = lax.cond(any_tie, tie_search, lambda _: jnp.full((1, nq), 2 ** 30, I32), 0)
    return thr, bound


def _d_prompt_body(q_ref, qi_ref, wit_ref, k_ref, vt_ref, ki_ref, bank_ref, ot_ref,
                   key_ref, m_ref, l_ref, acc_ref, s_ref, p_ref, a_ref, *, k_top, n_bank):
    t = Q_TILE
    blk = pl.program_id(1)
    n_chunks = blk + 1
    krow = lax.broadcasted_iota(I32, (t, t), 0)
    qcol = lax.broadcasted_iota(I32, (t, t), 1)
    qi = qi_ref[0]
    wit = wit_ref[0] * (D_IDX_HEADS ** -0.5) * (D_IDX_DIM ** -0.5)

    def score_chunk(c, carry):
        off = pl.multiple_of(c * t, t)
        ki = ki_ref[0, pl.ds(off, t), :][:, :D_IDX_DIM].astype(BF16)
        dots = _nt(ki, qi)
        sc = jnp.zeros((t, t), F32)
        for h in range(D_IDX_HEADS):
            sc = sc + wit[h:h + 1, :] * jnp.maximum(dots[:, h * t:(h + 1) * t], 0.0)
        key = _sort_key(sc)
        key_ref[pl.ds(off, t), :] = jnp.where((c < blk) | (krow <= qcol), key, jnp.int32(INT_MIN))
        return carry

    lax.fori_loop(0, n_chunks, score_chunk, 0)
    n_groups = (n_chunks + CNT_UNROLL - 1) // CNT_UNROLL

    def pad_chunk(c, carry):
        key_ref[pl.ds(pl.multiple_of(c * t, t), t), :] = jnp.full((t, t), INT_MIN, I32)
        return carry

    lax.fori_loop(n_chunks, n_groups * CNT_UNROLL, pad_chunk, 0)
    thr, bound = _kth_largest(key_ref, n_groups, k_top)
    thr = jnp.maximum(thr, jnp.int32(INT_MIN + 1))

    m_ref[...] = jnp.full(m_ref.shape, NEG, F32)
    l_ref[...] = jnp.zeros(l_ref.shape, F32)
    acc_ref[...] = jnp.zeros(acc_ref.shape, F32)
    q = (q_ref[0] * (D_HD ** -0.5)).astype(BF16)
    rep = D_HEADS // D_KV_HEADS

    def attend_chunk(c, carry):
        off = pl.multiple_of(c * t, t)
        key = key_ref[pl.ds(off, t), :]
        sel = (key > thr) | ((key == thr) & (krow + off < bound))
        kc = k_ref[0, pl.ds(off, t), :]
        vtc = vt_ref[0, :, pl.ds(off, t)]
        d = jnp.minimum(blk - c, n_bank - 1)
        for h in range(D_HEADS):
            gk = h // rep
            s_ref[h] = _nt(kc[:, gk * D_HD:(gk + 1) * D_HD], q[:, h * D_HD:(h + 1) * D_HD])
        for h in range(D_HEADS):
            s = jnp.where(sel, s_ref[h] + bank_ref[d, h], NEG)
            s_ref[h] = s
            m_old = m_ref[h]
            m_new = jnp.maximum(m_old, jnp.max(s, axis=0, keepdims=True))
            a_ref[h] = jnp.exp(m_old - m_new)
            m_ref[h] = m_new
        for h in range(D_HEADS):
            p = jnp.where(sel, jnp.exp(s_ref[h] - m_ref[h]), 0.0)
            l_ref[h] = a_ref[h] * l_ref[h] + jnp.sum(p, axis=0, keepdims=True)
            p_ref[h] = p.astype(BF16)
        for h in range(D_HEADS):
            gk = h // rep
            acc_ref[h] = a_ref[h] * acc_ref[h] + _nn(vtc[gk * D_HD:(gk + 1) * D_HD, :], p_ref[h])
        return carry

    lax.fori_loop(0, n_chunks, attend_chunk, 0)
    for h in range(D_HEADS):
        ot_ref[0, h * D_HD:(h + 1) * D_HD, :] = (acc_ref[h] / l_ref[h]).astype(ot_ref.dtype)


def _d_prompt(h3, qir, wit, kb, vt, bank, k_top):
    bsz, seq, _ = h3.shape
    t = Q_TILE
    assert seq % (CNT_UNROLL * t) == 0
    n_bank = bank.shape[0]
    body = functools.partial(_d_prompt_body, k_top=k_top, n_bank=n_bank)
    qw = D_HEADS * D_HD
    kw = D_KV_HEADS * D_HD
    return pl.pallas_call(
        body,
        grid=(bsz, seq // t),
        in_specs=[pl.BlockSpec((1, t, qw), lambda b, i: (b, i, 0)),
                  pl.BlockSpec((1, D_IDX_HEADS * t, D_IDX_DIM), lambda b, i: (b, i, 0)),
                  pl.BlockSpec((1, D_IDX_HEADS, t), lambda b, i: (b, 0, i)),
                  pl.BlockSpec((1, seq, kw), lambda b, i: (b, 0, 0)),
                  pl.BlockSpec((1, kw, seq), lambda b, i: (b, 0, 0)),
                  pl.BlockSpec((1, seq, LANES), lambda b, i: (b, 0, D_KI0 // LANES)),
                  pl.BlockSpec(bank.shape, lambda b, i: (0, 0, 0, 0), pipeline_mode=pl.Buffered(1))],
        out_specs=pl.BlockSpec((1, qw, t), lambda b, i: (b, 0, i)),
        out_shape=jax.ShapeDtypeStruct((bsz, qw, seq), BF16),
        scratch_shapes=[pltpu.VMEM((seq, t), I32),
                        pltpu.VMEM((D_HEADS, 1, t), F32), pltpu.VMEM((D_HEADS, 1, t), F32),
                        pltpu.VMEM((D_HEADS, D_HD, t), F32),
                        pltpu.VMEM((D_HEADS, t, t), F32), pltpu.VMEM((D_HEADS, t, t), BF16),
                        pltpu.VMEM((D_HEADS, 1, t), F32)],
        compiler_params=_cp("parallel", "arbitrary"),
        name="d_prompt",
    )(h3, qir, wit, kb, vt, h3, bank)


D_PG = 16


def _d_index_body(pt_ref, *refs):
    page_refs = refs[:D_PG]
    qi_ref, wi_ref, kin_ref, sc_ref, self_ref = refs[D_PG:]
    j = pl.program_id(1)
    qi = qi_ref[0]
    wi = wi_ref[0] * (D_IDX_HEADS ** -0.5)
    qb = qi.astype(BF16)
    parts = []
    for r in page_refs:
        dots = _nn(qb, r[0].astype(BF16)) * (D_IDX_DIM ** -0.5)
        parts.append(jnp.sum(wi * jnp.maximum(dots, 0.0), axis=0, keepdims=True))
    sc_ref[0] = jnp.concatenate(parts, axis=1)

    @pl.when(j == pl.num_programs(1) - 1)
    def _():
        kin = kin_ref[0].astype(BF16).astype(F32)
        dots = jnp.sum(qb.astype(F32) * kin, axis=1, keepdims=True) * (D_IDX_DIM ** -0.5)
        own = jnp.sum(wi * jnp.maximum(dots, 0.0), axis=0, keepdims=True)
        lane = lax.broadcasted_iota(I32, (1, LANES), 1)
        self_ref[0] = jnp.where(lane == 0, own, -jnp.inf)


def _d_index(page_flat, n_pages, kidx_t, qi3, wi3, kin3):
    nb = qi3.shape[0]
    steps = n_pages // D_PG

    def page_spec(i):
        return pl.BlockSpec((1, D_IDX_DIM, PAGE), lambda b, j, pt: (pt[b * n_pages + j * D_PG + i], 0, 0))

    per = lambda shp: pl.BlockSpec((1,) + shp, lambda b, j, pt: (b, 0, 0))
    gs = pltpu.PrefetchScalarGridSpec(
        num_scalar_prefetch=1,
        grid=(nb, steps),
        in_specs=[page_spec(i) for i in range(D_PG)]
        + [per((D_IDX_HEADS, D_IDX_DIM)), per((D_IDX_HEADS, 1)), per((1, D_IDX_DIM))],
        out_specs=[pl.BlockSpec((1, 1, D_PG * PAGE), lambda b, j, pt: (b, 0, j)),
                   per((1, LANES))],
    )
    return pl.pallas_call(
        _d_index_body,
        grid_spec=gs,
        out_shape=[jax.ShapeDtypeStruct((nb, 1, n_pages * PAGE), F32),
                   jax.ShapeDtypeStruct((nb, 1, LANES), F32)],
        compiler_params=_cp("parallel", "arbitrary"),
        name="d_index",
    )(page_flat, *([kidx_t] * D_PG), qi3, wi3, kin3)


def _d_thr_body(sc_ref, thr_ref, bound_ref, key_ref, *, k_top):
    n_chunks = sc_ref.shape[0] // LANES

    def fill(c, carry):
        off = pl.multiple_of(c * LANES, LANES)
        key_ref[pl.ds(off, LANES), :] = _sort_key(sc_ref[pl.ds(off, LANES), :])
        return carry

    lax.fori_loop(0, n_chunks, fill, 0)
    thr, bound = _kth_largest(key_ref, n_chunks // CNT_UNROLL, k_top)
    thr_ref[...] = thr
    bound_ref[...] = bound


def _d_thr(scores_t, k_top):
    nk, nq = scores_t.shape
    return pl.pallas_call(
        functools.partial(_d_thr_body, k_top=k_top),
        grid=(1,),
        in_specs=[pl.BlockSpec((nk, nq), lambda i: (0, 0))],
        out_specs=[pl.BlockSpec((1, nq), lambda i: (0, 0))] * 2,
        out_shape=[jax.ShapeDtypeStruct((1, nq), I32)] * 2,
        scratch_shapes=[pltpu.VMEM((nk, nq), I32)],
        compiler_params=_cp("arbitrary"),
        name="d_thr",
    )(scores_t)


def _d_decode_body(pt_ref, thr_ref, bnd_ref, *refs):
    kv_refs = refs[:D_PG]
    sc_ref, self_ref, q_ref, kvn_ref, g_ref, o_ref, m_ref, l_ref, acc_ref = refs[D_PG:]
    b = pl.program_id(0)
    j = pl.program_id(1)
    thr = thr_ref[b]
    bound = bnd_ref[b]
    kw = D_KV_HEADS * D_HD
    rep = D_HEADS // D_KV_HEADS
    hrow = lax.broadcasted_iota(I32, (D_HEADS, kw), 0)
    hcol = lax.broadcasted_iota(I32, (D_HEADS, kw), 1)
    diag = (hcol // D_HD) == (hrow // rep)

    @pl.when(j == 0)
    def _():
        m_ref[...] = jnp.full(m_ref.shape, NEG, F32)
        l_ref[...] = jnp.zeros(l_ref.shape, F32)
        acc_ref[...] = jnp.zeros(acc_ref.shape, F32)

    q = q_ref[0]
    qbd = jnp.where(diag, jnp.concatenate([q] * D_KV_HEADS, axis=1), 0.0).astype(BF16)
    kt = jnp.concatenate([r[0, 0] for r in kv_refs], axis=1).astype(BF16)
    vt = jnp.concatenate([r[0, 1] for r in kv_refs], axis=1).astype(BF16)
    width = D_PG * PAGE
    off = pl.multiple_of(j * width, width)
    key = _sort_key(sc_ref[0])
    pos = off + lax.broadcasted_iota(I32, (1, width), 1)
    sel = (key > thr) | ((key == thr) & (pos < bound))
    s = _nn(qbd, kt) * (D_HD ** -0.5) + g_ref[:, pl.ds(off, width)]
    s = jnp.where(sel, s, NEG)
    m_old = m_ref[...]
    m_new = jnp.maximum(m_old, jnp.max(s, axis=-1, keepdims=True))
    alpha = jnp.exp(m_old - m_new)
    p = jnp.where(sel, jnp.exp(s - m_new), 0.0)
    l_ref[...] = alpha * l_ref[...] + jnp.sum(p, axis=-1, keepdims=True)
    acc_ref[...] = alpha * acc_ref[...] + _nt(p.astype(BF16), vt)
    m_ref[...] = m_new

    @pl.when(j == pl.num_programs(1) - 1)
    def _():
        past = pl.num_programs(1) * width
        kn = kvn_ref[0, :, :kw].astype(BF16).astype(F32)
        vn = kvn_ref[0, :, kw:].astype(BF16).astype(F32)
        qf = qbd.astype(F32)
        s_self = (jnp.sum(qf * kn, axis=1, keepdims=True) * (D_HD ** -0.5)
                  + g_ref[:, pl.ds(pl.multiple_of(past, LANES), LANES)][:, 0:1])
        key_s = _sort_key(self_ref[0][:, 0:1])
        sel_s = (key_s > thr) | ((key_s == thr) & (past < bound))
        s_self = jnp.where(sel_s, s_self, NEG)
        m_o = m_ref[...]
        m_n = jnp.maximum(m_o, s_self)
        a = jnp.exp(m_o - m_n)
        p_s = jnp.where(sel_s, jnp.exp(s_self - m_n), 0.0)
        l = a * l_ref[...] + p_s
        acc = a * acc_ref[...] + p_s.astype(BF16).astype(F32) * vn
        o_full = jnp.where(diag, acc / l, 0.0)
        o_ref[0] = (o_full[:, 0:D_HD] + o_full[:, D_HD:2 * D_HD]
                    + o_full[:, 2 * D_HD:3 * D_HD] + o_full[:, 3 * D_HD:]).astype(o_ref.dtype)


def _d_decode(page_flat, thr, bound, n_pages, kv_t, scores3, self3, q3, kvn3, gtab):
    nb = q3.shape[0]
    steps = n_pages // D_PG
    kw = D_KV_HEADS * D_HD

    def kv_spec(i):
        return pl.BlockSpec((1, 2, kw, PAGE),
                            lambda b, j, pt, th, bd: (pt[b * n_pages + j * D_PG + i], 0, 0, 0))

    per = lambda shp: pl.BlockSpec((1,) + shp, lambda b, j, pt, th, bd: (b, 0, 0))
    gs = pltpu.PrefetchScalarGridSpec(
        num_scalar_prefetch=3,
        grid=(nb, steps),
        in_specs=[kv_spec(i) for i in range(D_PG)]
        + [pl.BlockSpec((1, 1, D_PG * PAGE), lambda b, j, pt, th, bd: (b, 0, j)),
           per((1, LANES)), per((D_HEADS, D_HD)), per((1, 2 * kw)),
           pl.BlockSpec(gtab.shape, lambda b, j, pt, th, bd: (0, 0))],
        out_specs=per((D_HEADS, D_HD)),
        scratch_shapes=[pltpu.VMEM((D_HEADS, 1), F32), pltpu.VMEM((D_HEADS, 1), F32),
                        pltpu.VMEM((D_HEADS, kw), F32)],
    )
    return pl.pallas_call(
        _d_decode_body,
        grid_spec=gs,
        out_shape=jax.ShapeDtypeStruct((nb, D_HEADS, D_HD), BF16),
        compiler_params=_cp("parallel", "arbitrary"),
        name="d_decode",
    )(page_flat, thr, bound, *([kv_t] * D_PG), scores3, self3, q3, kvn3, gtab)


def _d_bias_by_dist(rel_bias, n):
    return rel_bias[:, :D_HEADS][_rel_bucket(jnp.arange(n, dtype=I32))].astype(F32).T


def _mixer_d(xp, xs, bsz, seq, kv_cache, kidx_cache, page_table, rel_bias, g, d_w_in, d_w_out):
    nb = xs.shape[0]
    n_pages = page_table.shape[1]
    past = n_pages * PAGE
    w_in = jnp.pad(d_w_in, ((0, 0), (0, D_HW - d_w_in.shape[1]))).astype(BF16)
    w_out = d_w_out.astype(BF16)
    kw = D_KV_HEADS * D_HD

    h_p = _mm(xp, w_in, g=g)
    h3 = h_p.reshape(bsz, seq, D_HW)
    t = Q_TILE
    n_bank = min(seq // t, -(-(REL_MAX_DIST + t) // t) + 1)
    vec = _d_bias_by_dist(rel_bias, n_bank * t)
    w = jnp.concatenate([jnp.zeros((D_HEADS, t - 1), F32), vec, jnp.zeros((D_HEADS, 1), F32)], axis=1)
    hank = _hankel(w, t, n_bank * t)[:, ::-1, :]
    bank = jnp.transpose(hank.reshape(D_HEADS, t, n_bank, t), (2, 0, 1, 3))
    wit = jnp.transpose(h3[:, :, D_WI0:D_WI0 + D_IDX_HEADS], (0, 2, 1))
    kb = h3[:, :, D_KV0:D_KV0 + kw].astype(BF16)
    vt = jnp.transpose(h3[:, :, D_KV0 + kw:D_QI0], (0, 2, 1)).astype(BF16)
    qir = jnp.transpose(h3[:, :, D_QI0:D_KI0].reshape(bsz, seq // t, t, D_IDX_HEADS, D_IDX_DIM),
                        (0, 1, 3, 2, 4)).reshape(bsz, seq * D_IDX_HEADS, D_IDX_DIM).astype(BF16)
    ot = _d_prompt(h3, qir, wit, kb, vt, bank, min(D_TOPK_MAX, seq // 4))
    xp = _mm(jnp.transpose(ot, (0, 2, 1)).reshape(bsz * seq, -1), w_out, res=xp)
    kv_p = h3[:, :, D_KV0:D_QI0].reshape(bsz, seq, 2, D_KV_HEADS, D_HD)
    kidx_p = h3[:, :, D_KI0:D_WI0]

    h_s = _mm(xs, w_in, g=g)
    page_flat = page_table.reshape(-1)
    qi3 = h_s[:, D_QI0:D_KI0].reshape(nb, D_IDX_HEADS, D_IDX_DIM)
    wi3 = h_s[:, D_WI0:D_WI0 + D_IDX_HEADS].reshape(nb, D_IDX_HEADS, 1)
    kin3 = h_s[:, D_KI0:D_WI0].reshape(nb, 1, D_IDX_DIM)
    sc3, self3 = _d_index(page_flat, n_pages, jnp.transpose(kidx_cache, (0, 2, 1)), qi3, wi3, kin3)
    n_rows = -(-(past + LANES) // (CNT_UNROLL * LANES)) * (CNT_UNROLL * LANES)
    scores_t = jnp.concatenate([sc3.reshape(nb, past), self3.reshape(nb, LANES),
                                jnp.full((nb, n_rows - past - LANES), -jnp.inf, F32)], axis=1).T
    thr, bound = _d_thr(scores_t, min(D_TOPK_MAX, (past + 1) // 4))
    vec_s = _d_bias_by_dist(rel_bias, past + 1)
    gtab_s = jnp.concatenate([vec_s[:, :0:-1], jnp.broadcast_to(vec_s[:, 0:1], (D_HEADS, LANES))], axis=1)
    kv_t = jnp.transpose(kv_cache, (0, 2, 3, 4, 1)).reshape(kv_cache.shape[0], 2, kw, PAGE)
    o_s = _d_decode(page_flat, thr[0], bound[0], n_pages, kv_t, sc3, self3,
                    h_s[:, :D_KV0].reshape(nb, D_HEADS, D_HD), h_s[:, D_KV0:D_QI0].reshape(nb, 1, 2 * kw), gtab_s)
    xs = _mm(o_s.reshape(nb, -1), w_out, res=xs)
    kv_s = h_s[:, D_KV0:D_QI0].reshape(nb, 1, 2, D_KV_HEADS, D_HD)
    kidx_s = h_s[:, D_KI0:D_WI0].reshape(nb, 1, D_IDX_DIM)
    return xp, xs, kv_p, kidx_p, kv_s, kidx_s


def kernel(x_prompt, x_sample, mem_prompt, cache_a1_kv, cache_a2_kv, cache_a3_kv, state_b_conv,
           cache_c_latent, cache_c_krope, cache_d_kv, cache_d_kidx, cache_mem_kv, page_table,
           rel_bias, g_mix, g_cross, g_ffn, g_final, w_xq, w_xkv, w_xo, w_ffn_in, w_ffn_out,
           a_w_in, a_w_out, b_w_pw1, b_b_pw1, b_w_dw, b_b_dw, b_ln_g, b_ln_b, b_w_pw2, b_b_pw2,
           c_w_down, c_g_q, c_g_kv, c_w_uq, c_w_uk, c_w_uv, c_w_out, d_w_in, d_w_out):
    bsz, seq, d = x_prompt.shape
    nb = x_sample.shape[0]
    assert x_sample.shape[1] == 1
    depth = g_mix.shape[0]
    xp = x_prompt.reshape(bsz * seq, d)
    xs = x_sample.reshape(nb, d)
    mem2 = mem_prompt.reshape(bsz * MEM_LEN, d)
    hw = X_HEADS * X_HD
    mem5 = cache_mem_kv.reshape(depth, nb, MEM_LEN, 2 * X_HEADS, X_HD)
    mem_kv_out = []
    outs = {}
    for i in range(depth):
        kind = i % 4
        if kind == 0:
            xp, xs, a_p, a_s = _mixer_a(xp, xs, bsz, seq, [cache_a1_kv, cache_a2_kv, cache_a3_kv],
                                        rel_bias, g_mix[i], a_w_in, a_w_out)
            outs["a_p"], outs["a_s"] = a_p, a_s
        elif kind == 1:
            xp, xs, conv_p, conv_s = _mixer_b(xp, xs, bsz, seq, state_b_conv, g_mix[i], b_w_pw1, b_b_pw1,
                                              b_w_dw, b_b_dw, b_ln_g, b_ln_b, b_w_pw2, b_b_pw2)
            outs["conv"] = (conv_p, conv_s)
        elif kind == 2:
            xp, xs, lat_p, kr_p, lat_s, kr_s = _mixer_c(xp, xs, bsz, seq, cache_c_latent, cache_c_krope,
                                                        page_table, g_mix[i], c_w_down, c_g_q, c_g_kv,
                                                        c_w_uq, c_w_uk, c_w_uv, c_w_out)
            outs["c"] = (lat_p, kr_p, lat_s, kr_s)
        else:
            xp, xs, kv_p, kidx_p, kv_s, kidx_s = _mixer_d(xp, xs, bsz, seq, cache_d_kv, cache_d_kidx,
                                                          page_table, rel_bias, g_mix[i], d_w_in, d_w_out)
            outs["d"] = (kv_p, kidx_p, kv_s, kidx_s)
        mkv = _mm(mem2, w_xkv[i].astype(BF16))
        mem_kv_out.append(mkv.reshape(bsz, MEM_LEN, 2, X_HEADS, X_HD))
        wq = w_xq[i].astype(BF16)
        wo = w_xo[i].astype(BF16)
        xp = _cross(xp.reshape(bsz, seq, d), g_cross[i], wq, wo, mkv.reshape(bsz, MEM_LEN, 2 * hw)).reshape(bsz * seq, d)
        q_s = _mm(xs, wq, g=g_cross[i]).reshape(nb, X_HEADS, X_HD)
        xs = _mm(_cross_s(q_s, mem5, i).reshape(nb, hw), wo, res=xs)
        w_in = w_ffn_in[i].astype(BF16)
        w_out = w_ffn_out[i].astype(BF16)
        xp = _ffn(xp, g_ffn[i], w_in, w_out)
        xs = _ffn(xs, g_ffn[i], w_in, w_out)
    y_p = _rmsnorm(xp, g_final).reshape(bsz, seq, d)
    y_s = _rmsnorm(xs, g_final).reshape(nb, 1, d)
    a_p, a_s = outs["a_p"], outs["a_s"]
    conv_p, conv_s = outs["conv"]
    lat_p, kr_p, lat_s, kr_s = outs["c"]
    kv_p, kidx_p, kv_s, kidx_s = outs["d"]
    return (y_p, y_s, a_p[0], a_p[1], a_p[2], a_s[0], a_s[1], a_s[2], conv_p, conv_s,
            lat_p, kr_p, lat_s, kr_s, kv_p, kidx_p, kv_s, kidx_s, jnp.stack(mem_kv_out))
```

```python
import functools
import math

import jax
import jax.numpy as jnp
import numpy as np
from jax import lax
from jax.experimental import pallas as pl
from jax.experimental.pallas import tpu as pltpu

F32 = jnp.float32
BF16 = jnp.bfloat16
I32 = jnp.int32

EPS = 1e-6
PAGE = 128
LANES = 128
VMEM_LIMIT = 52 * 1024 * 1024
NEG = -1e30
M_INIT = -5e29

REL_BUCKETS = 32
REL_MAX_DIST = 2048
A_WINDOWS = (128, 512, 2048)
A_DILATIONS = (1, 4, 16)
A_HEADS = 8
A_HD = 64
A_NKEYS = 129
B_CONV_WIDTH = 31
C_HEADS = 16
C_Q_LORA = 384
C_KV_LORA = 256
C_NOPE = 64
C_ROPE = 32
C_V = 64
ROPE_THETA = 10000.0
D_HEADS = 16
D_KV_HEADS = 4
D_HD = 64
D_IDX_HEADS = 8
D_IDX_DIM = 64
D_TOPK_MAX = 256
X_HEADS = 4
X_HD = 128
MEM_LEN = 256
Q_TILE = 128


def _cp(*sem):
    return pltpu.CompilerParams(dimension_semantics=sem, vmem_limit_bytes=VMEM_LIMIT)


def _nt(a, b):
    return lax.dot_general(a, b, (((1,), (1,)), ((), ())), preferred_element_type=F32)


def _nn(a, b):
    return jnp.dot(a, b, preferred_element_type=F32)


def _sigmoid(x):
    return 1.0 / (1.0 + jnp.exp(-x))


def _rms(x, g):
    return x * lax.rsqrt(jnp.mean(x * x, axis=-1, keepdims=True) + EPS) * g


def _rel_bucket(dist):
    n = jnp.maximum(dist, 0)
    max_exact = REL_BUCKETS // 2
    nf = jnp.maximum(n, 1).astype(F32)
    large = max_exact + (jnp.log(nf / max_exact) / math.log(REL_MAX_DIST / max_exact)
                         * (REL_BUCKETS - max_exact)).astype(I32)
    large = jnp.minimum(large, REL_BUCKETS - 1)
    return jnp.where(n < max_exact, n, large)


def _mm_body(*refs, norm, bias, glu, resid):
    it = iter(refs)
    x_ref = next(it)
    w_ref = next(it)
    g_ref = next(it) if norm else None
    b_ref = next(it) if bias else None
    w2_ref = next(it) if glu else None
    b2_ref = next(it) if (glu and bias) else None
    r_ref = next(it) if resid else None
    o_ref = next(it)
    xn_ref = next(it)

    @pl.when(pl.program_id(1) == 0)
    def _():
        x = x_ref[...].astype(F32)
        if norm:
            x = _rms(x, g_ref[...])
        xn_ref[...] = x.astype(BF16)

    xn = xn_ref[...]
    h = _nn(xn, w_ref[...])
    if bias:
        h = h + b_ref[...]
    if glu:
        h2 = _nn(xn, w2_ref[...])
        if bias:
            h2 = h2 + b2_ref[...]
        h = h * _sigmoid(h2)
    if resid:
        h = h + r_ref[...]
    o_ref[...] = h.astype(o_ref.dtype)


def _mm(x, w, *, g=None, b=None, w2=None, b2=None, res=None, out_dtype=F32, tm=512, tn=None):
    n, k = x.shape
    m = w.shape[1]
    tm = min(tm, n)
    tn = tn or m
    assert n % tm == 0 and m % tn == 0
    args = [x, w]
    specs = [pl.BlockSpec((tm, k), lambda i, j: (i, 0)),
             pl.BlockSpec((k, tn), lambda i, j: (0, j))]
    if g is not None:
        args.append(g.reshape(1, k))
        specs.append(pl.BlockSpec((1, k), lambda i, j: (0, 0)))
    if b is not None:
        args.append(b.reshape(1, m))
        specs.append(pl.BlockSpec((1, tn), lambda i, j: (0, j)))
    if w2 is not None:
        args.append(w2)
        specs.append(pl.BlockSpec((k, tn), lambda i, j: (0, j)))
        if b2 is not None:
            args.append(b2.reshape(1, m))
            specs.append(pl.BlockSpec((1, tn), lambda i, j: (0, j)))
    if res is not None:
        args.append(res)
        specs.append(pl.BlockSpec((tm, tn), lambda i, j: (i, j)))
    body = functools.partial(_mm_body, norm=g is not None, bias=b is not None,
                             glu=w2 is not None, resid=res is not None)
    return pl.pallas_call(
        body,
        grid=(n // tm, m // tn),
        in_specs=specs,
        out_specs=pl.BlockSpec((tm, tn), lambda i, j: (i, j)),
        out_shape=jax.ShapeDtypeStruct((n, m), out_dtype),
        scratch_shapes=[pltpu.VMEM((tm, k), BF16)],
        compiler_params=_cp("parallel", "arbitrary"),
        name="mm",
    )(*args)


def _ffn_body(x_ref, g_ref, wg_ref, wu_ref, wo_ref, o_ref, xn_ref):
    j = pl.program_id(1)

    @pl.when(j == 0)
    def _():
        x = x_ref[...]
        xn_ref[...] = _rms(x, g_ref[...]).astype(BF16)
        o_ref[...] = x

    xn = xn_ref[...]
    hg = _nn(xn, wg_ref[...])
    hu = _nn(xn, wu_ref[...])
    a = (hg * _sigmoid(hg) * hu).astype(BF16)
    o_ref[...] += _nn(a, wo_ref[...])


def _ffn(x, g, w_in, w_out, *, tm=512):
    n, d = x.shape
    dff = w_out.shape[0]
    tf = dff // 2
    assert tf % LANES == 0
    tm = min(tm, n)
    nf = dff // tf
    return pl.pallas_call(
        _ffn_body,
        grid=(n // tm, nf),
        in_specs=[pl.BlockSpec((tm, d), lambda i, j: (i, 0)),
                  pl.BlockSpec((1, d), lambda i, j: (0, 0)),
                  pl.BlockSpec((d, tf), lambda i, j: (0, j)),
                  pl.BlockSpec((d, tf), lambda i, j: (0, j + nf)),
                  pl.BlockSpec((tf, d), lambda i, j: (j, 0))],
        out_specs=pl.BlockSpec((tm, d), lambda i, j: (i, 0)),
        out_shape=jax.ShapeDtypeStruct((n, d), F32),
        scratch_shapes=[pltpu.VMEM((tm, d), BF16)],
        compiler_params=_cp("parallel", "arbitrary"),
        name="ffn",
    )(x, g.reshape(1, d), w_in, w_in, w_out)


def _cross_body(x_ref, g_ref, wq_ref, wo_ref, kv_ref, o_ref):
    x = x_ref[0]
    xn = _rms(x, g_ref[...]).astype(BF16)
    q = _nn(xn, wq_ref[...]).astype(BF16)
    hw = X_HEADS * X_HD
    outs = []
    for h in range(X_HEADS):
        kh = kv_ref[0, :, h * X_HD:(h + 1) * X_HD].astype(BF16)
        vh = kv_ref[0, :, hw + h * X_HD:hw + (h + 1) * X_HD].astype(BF16)
        s = _nt(q[:, h * X_HD:(h + 1) * X_HD], kh) * (X_HD ** -0.5)
        m = jnp.max(s, axis=-1, keepdims=True)
        p = jnp.exp(s - m)
        l = jnp.sum(p, axis=-1, keepdims=True)
        outs.append(_nn((p / l).astype(BF16), vh))
    o = jnp.concatenate(outs, axis=-1).astype(BF16)
    o_ref[0] = x + _nn(o, wo_ref[...])


def _cross(x3, g, wq, wo, kv3, *, tq=512):
    bsz, t, d = x3.shape
    tq = min(tq, t)
    hw = X_HEADS * X_HD
    return pl.pallas_call(
        _cross_body,
        grid=(bsz, t // tq),
        in_specs=[pl.BlockSpec((1, tq, d), lambda b, i: (b, i, 0)),
                  pl.BlockSpec((1, d), lambda b, i: (0, 0)),
                  pl.BlockSpec((d, hw), lambda b, i: (0, 0)),
                  pl.BlockSpec((hw, d), lambda b, i: (0, 0)),
                  pl.BlockSpec((1, MEM_LEN, 2 * hw), lambda b, i: (b, 0, 0))],
        out_specs=pl.BlockSpec((1, tq, d), lambda b, i: (b, i, 0)),
        out_shape=jax.ShapeDtypeStruct((bsz, t, d), F32),
        compiler_params=_cp("parallel", "parallel"),
        name="cross",
    )(x3, g.reshape(1, d), wq, wo, kv3)


def _cross_s_body(q_ref, kv_ref, o_ref):
    def one(j, carry):
        k = kv_ref[0, j, :, 0:X_HEADS, :]
        v = kv_ref[0, j, :, X_HEADS:2 * X_HEADS, :]
        q = q_ref[j]
        s = jnp.sum(k * q[None], axis=-1, keepdims=True) * (X_HD ** -0.5)
        m = jnp.max(s, axis=0, keepdims=True)
        p = jnp.exp(s - m)
        l = jnp.sum(p, axis=0, keepdims=True)
        o_ref[j] = jnp.sum((p / l) * v, axis=0)
        return carry

    lax.fori_loop(0, q_ref.shape[0], one, 0)


def _cross_s(q3, mem_kv5, layer, *, tb=8):
    nb = q3.shape[0]
    tb = min(tb, nb)
    return pl.pallas_call(
        _cross_s_body,
        grid=(nb // tb,),
        in_specs=[pl.BlockSpec((tb, X_HEADS, X_HD), lambda b: (b, 0, 0)),
                  pl.BlockSpec((1, tb, MEM_LEN, 2 * X_HEADS, X_HD), lambda b: (layer, b, 0, 0, 0))],
        out_specs=pl.BlockSpec((tb, X_HEADS, X_HD), lambda b: (b, 0, 0)),
        out_shape=jax.ShapeDtypeStruct((nb, X_HEADS, X_HD), F32),
        compiler_params=_cp("parallel"),
        name="cross_s",
    )(q3, mem_kv5)


def _rms_body(x_ref, g_ref, o_ref):
    o_ref[...] = _rms(x_ref[...], g_ref[...])


def _rmsnorm(x, g, *, tm=512):
    n, d = x.shape
    tm = min(tm, n)
    return pl.pallas_call(
        _rms_body,
        grid=(n // tm,),
        in_specs=[pl.BlockSpec((tm, d), lambda i: (i, 0)),
                  pl.BlockSpec((1, d), lambda i: (0, 0))],
        out_specs=pl.BlockSpec((tm, d), lambda i: (i, 0)),
        out_shape=jax.ShapeDtypeStruct((n, d), F32),
        compiler_params=_cp("parallel"),
        name="rmsnorm",
    )(x, g.reshape(1, d))


def _a_bias_vec(rel_bias, g):
    dist = A_DILATIONS[g] * jnp.arange(A_NKEYS, dtype=I32)
    return rel_bias[_rel_bucket(dist)][:, g * A_HEADS:(g + 1) * A_HEADS].astype(F32).T


def _a_prompt_body(q_ref, kc_ref, kp_ref, vc_ref, vp_ref, bias_ref, o_ref, lse_ref):
    j = pl.program_id(1)
    q = q_ref[0].astype(BF16)
    k = jnp.concatenate([kp_ref[0], kc_ref[0]], axis=0).astype(BF16)
    v = jnp.concatenate([vp_ref[0], vc_ref[0]], axis=0).astype(BF16)
    t = Q_TILE
    row = lax.broadcasted_iota(I32, (t, 2 * t), 0)
    col = lax.broadcasted_iota(I32, (t, 2 * t), 1)
    back = row + t - col
    ok = (back >= 0) & (back <= t) & ((col >= t) | (j > 0))
    outs, lses = [], []
    for h in range(A_HEADS):
        sl = slice(h * A_HD, (h + 1) * A_HD)
        s = _nt(q[:, sl], k[:, sl]) * (A_HD ** -0.5) + bias_ref[h]
        s = jnp.where(ok, s, NEG)
        m = jnp.max(s, axis=-1, keepdims=True)
        p = jnp.exp(s - m)
        l = jnp.sum(p, axis=-1, keepdims=True)
        outs.append(_nn(p.astype(BF16), v[:, sl]) / l)
        lses.append(jnp.broadcast_to(m + jnp.log(l), (t, A_HD)))
    o_ref[0] = jnp.concatenate(outs, axis=-1)
    lse_ref[0] = jnp.concatenate(lses, axis=-1)


def _a_prompt_group(qkv_g, bias_tile):
    nr, sj, _ = qkv_g.shape
    hw = A_HEADS * A_HD
    t = Q_TILE
    assert sj % t == 0

    def cur(which):
        return pl.BlockSpec((1, t, hw), lambda r, j: (r, j, which))

    def prev(which):
        return pl.BlockSpec((1, t, hw), lambda r, j: (r, jnp.maximum(j - 1, 0), which))

    out_spec = pl.BlockSpec((1, t, hw), lambda r, j: (r, j, 0))
    return pl.pallas_call(
        _a_prompt_body,
        grid=(nr, sj // t),
        in_specs=[cur(0), cur(1), prev(1), cur(2), prev(2),
                  pl.BlockSpec((A_HEADS, t, 2 * t), lambda r, j: (0, 0, 0))],
        out_specs=[out_spec, out_spec],
        out_shape=[jax.ShapeDtypeStruct((nr, sj, hw), F32)] * 2,
        compiler_params=_cp("parallel", "parallel"),
        name="a_prompt",
    )(qkv_g, qkv_g, qkv_g, qkv_g, qkv_g, bias_tile)


def _toeplitz(w, n_rows, n_cols):
    h, p = w.shape
    assert n_cols <= p - 1
    flat = jnp.tile(w, (1, n_rows))[:, :n_rows * (p - 1)]
    return flat.reshape(h, n_rows, p - 1)[:, :, :n_cols]


def _a_bias_tile(vec):
    t = Q_TILE
    nh = vec.shape[0]
    w = jnp.concatenate([vec[:, t:t + 1], jnp.zeros((nh, 2 * t - 1), F32), vec[:, :t]], axis=1)
    return jnp.transpose(_toeplitz(w, 2 * t, t), (0, 2, 1))


def _a_combine_body(o0, o1, o2, l0, l1, l2, x_ref, w_ref, out_ref):
    a0, a1, a2 = l0[...], l1[...], l2[...]
    m = jnp.maximum(jnp.maximum(a0, a1), a2)
    e0, e1, e2 = jnp.exp(a0 - m), jnp.exp(a1 - m), jnp.exp(a2 - m)
    den = e0 + e1 + e2
    o = (e0 / den) * o0[...] + (e1 / den) * o1[...] + (e2 / den) * o2[...]
    out_ref[...] = x_ref[...] + _nn(o.astype(BF16), w_ref[...])


def _a_combine(os_, ls_, x, w_out, *, tm=512):
    n, d = x.shape
    hw = w_out.shape[0]
    tm = min(tm, n)
    small = pl.BlockSpec((tm, hw), lambda i: (i, 0))
    return pl.pallas_call(
        _a_combine_body,
        grid=(n // tm,),
        in_specs=[small] * 6 + [pl.BlockSpec((tm, d), lambda i: (i, 0)),
                                pl.BlockSpec((hw, d), lambda i: (0, 0))],
        out_specs=pl.BlockSpec((tm, d), lambda i: (i, 0)),
        out_shape=jax.ShapeDtypeStruct((n, d), F32),
        compiler_params=_cp("parallel"),
        name="a_combine",
    )(*os_, *ls_, x, w_out)


def _row_to_col(r):
    n = r.shape[1]
    eye = (lax.broadcasted_iota(I32, (LANES, LANES), 0) == lax.broadcasted_iota(I32, (LANES, LANES), 1))
    cols = []
    for c in range(n // LANES):
        blk = jnp.broadcast_to(r[:, c * LANES:(c + 1) * LANES], (LANES, LANES))
        cols.append(jnp.sum(jnp.where(eye, blk, 0.0), axis=1, keepdims=True))
    return jnp.concatenate(cols, axis=0)


def _col_to_row(c):
    n = c.shape[0]
    eye = (lax.broadcasted_iota(I32, (LANES, LANES), 0) == lax.broadcasted_iota(I32, (LANES, LANES), 1))
    rows = []
    for i in range(n // LANES):
        blk = jnp.broadcast_to(c[i * LANES:(i + 1) * LANES, :], (LANES, LANES))
        rows.append(jnp.sum(jnp.where(eye, blk, 0.0), axis=0, keepdims=True))
    return jnp.concatenate(rows, axis=1)


def _a_sample_body(q_ref, kvn_ref, bias_ref, c_ref, o_ref, lse_ref, cn_ref, *, width):
    hw = A_HEADS * A_HD
    q_col = _row_to_col(q_ref[0])
    kvn_row = kvn_ref[0]
    kvn_col = _row_to_col(kvn_row)
    scale = A_HD ** -0.5
    s_rows, self_rows = [], []
    for h in range(A_HEADS):
        sl = slice(h * A_HD, (h + 1) * A_HD)
        kt = c_ref[0, sl, :]
        s_rows.append(jnp.sum(kt * q_col[sl], axis=0, keepdims=True))
        self_rows.append(jnp.sum(kvn_col[sl] * q_col[sl], axis=0, keepdims=True))
    bias = bias_ref[...]
    s = jnp.concatenate(s_rows, axis=0) * scale + bias[:, :width]
    s_self = jnp.concatenate(self_rows, axis=0) * scale + bias[:, width:width + 1]
    m = jnp.maximum(jnp.max(s, axis=-1, keepdims=True), s_self)
    p = jnp.exp(s - m)
    p_self = jnp.exp(s_self - m)
    l = jnp.sum(p, axis=-1, keepdims=True) + p_self
    lse = m + jnp.log(l)
    o_cols, lse_cols = [], []
    for h in range(A_HEADS):
        sl = slice(hw + h * A_HD, hw + (h + 1) * A_HD)
        vt = c_ref[0, sl, :]
        oc = jnp.sum(vt * p[h:h + 1, :], axis=1, keepdims=True) + kvn_col[sl] * p_self[h:h + 1, :]
        o_cols.append(oc / l[h:h + 1, :])
        lse_cols.append(jnp.broadcast_to(lse[h:h + 1, :], (A_HD, 1)))
    o_ref[0] = _col_to_row(jnp.concatenate(o_cols, axis=0))
    lse_ref[0] = _col_to_row(jnp.concatenate(lse_cols, axis=0))
    rc = 128
    lane = lax.broadcasted_iota(I32, (rc, width), 1)
    for r0 in range(0, 2 * hw, rc):
        rolled = pltpu.roll(c_ref[0, r0:r0 + rc, :], width - 1, 1)
        cn_ref[0, r0:r0 + rc, :] = jnp.where(lane == width - 1, kvn_col[r0:r0 + rc], rolled)


def _a_sample_group(q_g, kvn_g, cache_t, bias_s):
    bsz, rows, width = cache_t.shape
    hw = A_HEADS * A_HD
    body = functools.partial(_a_sample_body, width=width)
    return pl.pallas_call(
        body,
        grid=(bsz,),
        in_specs=[pl.BlockSpec((1, 1, hw), lambda b: (b, 0, 0)),
                  pl.BlockSpec((1, 1, 2 * hw), lambda b: (b, 0, 0)),
                  pl.BlockSpec((A_HEADS, width + LANES), lambda b: (0, 0)),
                  pl.BlockSpec((1, rows, width), lambda b: (b, 0, 0))],
        out_specs=[pl.BlockSpec((1, 1, hw), lambda b: (b, 0, 0)),
                   pl.BlockSpec((1, 1, hw), lambda b: (b, 0, 0)),
                   pl.BlockSpec((1, rows, width), lambda b: (b, 0, 0))],
        out_shape=[jax.ShapeDtypeStruct((bsz, 1, hw), F32),
                   jax.ShapeDtypeStruct((bsz, 1, hw), F32),
                   jax.ShapeDtypeStruct((bsz, rows, width), F32)],
        compiler_params=_cp("parallel"),
        name="a_sample",
    )(q_g, kvn_g, bias_s, cache_t)


def _mixer_a(xp, xs, bsz, seq, caches, rel_bias, g, a_w_in, a_w_out):
    hw = A_HEADS * A_HD
    w_in = a_w_in.astype(BF16)
    w_out = a_w_out.astype(BF16)
    qkv_s = _mm(xs, w_in, g=g, tn=w_in.shape[1] // 2)
    os_p, ls_p, os_s, ls_s, new_p, new_s = [], [], [], [], [], []
    nb = xs.shape[0]
    dm = xp.shape[1]
    for gi in range(3):
        vec = _a_bias_vec(rel_bias, gi)
        d = A_DILATIONS[gi]
        sj = seq // d
        x_g = jnp.transpose(xp.reshape(bsz, sj, d, dm), (0, 2, 1, 3)).reshape(bsz * seq, dm) if d > 1 else xp
        w_g = jnp.concatenate([w_in[:, (3 * j + gi) * hw:(3 * j + gi + 1) * hw] for j in range(3)], axis=1)
        qkv_g = _mm(x_g, w_g, g=g).reshape(bsz * d, sj, 3 * hw)
        o, lse = _a_prompt_group(qkv_g, _a_bias_tile(vec))

        def natural(a):
            return jnp.transpose(a.reshape(bsz, d, sj, -1), (0, 2, 1, 3)).reshape(bsz * seq, -1)

        os_p.append(natural(o))
        ls_p.append(natural(lse))
        w = min(A_WINDOWS[gi], seq)
        assert w % d == 0
        kv_tail = qkv_g.reshape(bsz, d, sj, 3 * hw)[:, :, sj - w // d:, hw:]
        new_p.append(jnp.transpose(kv_tail, (0, 2, 1, 3)).reshape(bsz, w, 2, A_HEADS, A_HD))
        cache = caches[gi]
        width = cache.shape[1]
        assert width == (A_NKEYS - 1) * d
        cache_t = jnp.transpose(cache, (0, 2, 3, 4, 1)).reshape(nb, 2 * hw, width)
        q_s = qkv_s[:, gi * hw:(gi + 1) * hw].reshape(nb, 1, hw)
        kvn = jnp.concatenate([qkv_s[:, (3 + gi) * hw:(4 + gi) * hw],
                               qkv_s[:, (6 + gi) * hw:(7 + gi) * hw]], axis=1).reshape(nb, 1, 2 * hw)
        lane = jnp.arange(width)
        kk = (width - lane) // d
        bias_c = jnp.where((lane % d == 0)[None, :], vec[:, jnp.clip(kk, 0, A_NKEYS - 1)], NEG)
        bias_s = jnp.concatenate([bias_c, jnp.broadcast_to(vec[:, 0:1], (A_HEADS, LANES))], axis=1)
        o_s, lse_s, cache_new = _a_sample_group(q_s, kvn, cache_t, bias_s)
        os_s.append(o_s.reshape(nb, hw))
        ls_s.append(lse_s.reshape(nb, hw))
        new_s.append(jnp.transpose(cache_new.reshape(nb, 2, A_HEADS, A_HD, width), (0, 4, 1, 2, 3)))
    xp = _a_combine(os_p, ls_p, xp, w_out)
    xs = _a_combine(os_s, ls_s, xs, w_out)
    return xp, xs, new_p, new_s


def _ln_silu_proj(c, lng_ref, lnb_ref, w2_ref, b2_ref, x):
    mu = jnp.mean(c, axis=-1, keepdims=True)
    xc = c - mu
    y = xc * lax.rsqrt(jnp.mean(xc * xc, axis=-1, keepdims=True) + EPS)
    y = y * lng_ref[...] + lnb_ref[...]
    y = y * _sigmoid(y)
    return x + _nn(y.astype(BF16), w2_ref[...]) + b2_ref[...]


def _b_prompt_body(uc_ref, up_ref, x_ref, wdw_ref, bdw_ref, lng_ref, lnb_ref, w2_ref, b2_ref,
                   o_ref, buf_ref, c_ref, *, tq):
    i = pl.program_id(1)
    pad = 32
    buf_ref[0:pad, :] = jnp.where(i > 0, up_ref[0], 0.0)
    buf_ref[pad:pad + tq, :] = uc_ref[0]
    off = pad - (B_CONV_WIDTH - 1)
    rc, cc = 128, 256
    d = uc_ref.shape[2]
    for r0 in range(0, tq, rc):
        for c0 in range(0, d, cc):
            acc = jnp.zeros((rc, cc), F32)
            for w in range(B_CONV_WIDTH):
                acc = acc + buf_ref[r0 + off + w:r0 + off + w + rc, c0:c0 + cc] * wdw_ref[w:w + 1, c0:c0 + cc]
            c_ref[r0:r0 + rc, c0:c0 + cc] = acc
    c = c_ref[...] + bdw_ref[...]
    o_ref[0] = _ln_silu_proj(c, lng_ref, lnb_ref, w2_ref, b2_ref, x_ref[0])


def _b_prompt(u3, x3, wdw, bdw, lng, lnb, w2, b2, *, tq=256):
    bsz, s, d = u3.shape
    tq = min(tq, s)
    pad = 32
    body = functools.partial(_b_prompt_body, tq=tq)
    vec = lambda: pl.BlockSpec((1, d), lambda b, i: (0, 0))
    return pl.pallas_call(
        body,
        grid=(bsz, s // tq),
        in_specs=[pl.BlockSpec((1, tq, d), lambda b, i: (b, i, 0)),
                  pl.BlockSpec((1, pad, d), lambda b, i: (b, jnp.maximum(i * (tq // pad) - 1, 0), 0)),
                  pl.BlockSpec((1, tq, d), lambda b, i: (b, i, 0)),
                  pl.BlockSpec((B_CONV_WIDTH, d), lambda b, i: (0, 0)),
                  vec(), vec(), vec(),
                  pl.BlockSpec((d, d), lambda b, i: (0, 0)),
                  vec()],
        out_specs=pl.BlockSpec((1, tq, d), lambda b, i: (b, i, 0)),
        out_shape=jax.ShapeDtypeStruct((bsz, s, d), F32),
        scratch_shapes=[pltpu.VMEM((tq + pad, d), F32), pltpu.VMEM((tq, d), F32)],
        compiler_params=_cp("parallel", "parallel"),
        name="b_prompt",
    )(u3, u3, x3, wdw, bdw.reshape(1, d), lng.reshape(1, d), lnb.reshape(1, d), w2, b2.reshape(1, d))


def _b_sample_body(st_ref, u_ref, x_ref, wdw_ref, bdw_ref, lng_ref, lnb_ref, w2_ref, b2_ref,
                   o_ref, ns_ref):
    nw = B_CONV_WIDTH - 1
    u = u_ref[...]
    acc = u * wdw_ref[nw:nw + 1, :] + bdw_ref[...]
    for w in range(nw):
        acc = acc + st_ref[w] * wdw_ref[w:w + 1, :]
        if w > 0:
            ns_ref[w - 1] = st_ref[w]
    ns_ref[nw - 1] = u
    o_ref[...] = _ln_silu_proj(acc, lng_ref, lnb_ref, w2_ref, b2_ref, x_ref[...])


def _b_sample(state_t, u, x, wdw, bdw, lng, lnb, w2, b2, *, tb=32):
    nw, nb, d = state_t.shape
    tb = min(tb, nb)
    vec = lambda: pl.BlockSpec((1, d), lambda i: (0, 0))
    return pl.pallas_call(
        _b_sample_body,
        grid=(nb // tb,),
        in_specs=[pl.BlockSpec((nw, tb, d), lambda i: (0, i, 0)),
                  pl.BlockSpec((tb, d), lambda i: (i, 0)),
                  pl.BlockSpec((tb, d), lambda i: (i, 0)),
                  pl.BlockSpec((B_CONV_WIDTH, d), lambda i: (0, 0)),
                  vec(), vec(), vec(),
                  pl.BlockSpec((d, d), lambda i: (0, 0)),
                  vec()],
        out_specs=[pl.BlockSpec((tb, d), lambda i: (i, 0)),
                   pl.BlockSpec((nw, tb, d), lambda i: (0, i, 0))],
        out_shape=[jax.ShapeDtypeStruct((nb, d), F32),
                   jax.ShapeDtypeStruct((nw, nb, d), F32)],
        compiler_params=_cp("parallel"),
        name="b_sample",
    )(state_t, u, x, wdw, bdw.reshape(1, d), lng.reshape(1, d), lnb.reshape(1, d), w2, b2.reshape(1, d))


def _mixer_b(xp, xs, bsz, seq, state, g, w_pw1, b_pw1, w_dw, b_dw, ln_g, ln_b, w_pw2, b_pw2):
    d = xp.shape[1]
    wa = w_pw1[:, :d].astype(BF16)
    wb = w_pw1[:, d:].astype(BF16)
    w2 = w_pw2.astype(BF16)
    nw = B_CONV_WIDTH - 1
    u_p = _mm(xp, wa, g=g, b=b_pw1[:d], w2=wb, b2=b_pw1[d:])
    u_s = _mm(xs, wa, g=g, b=b_pw1[:d], w2=wb, b2=b_pw1[d:])
    u3 = u_p.reshape(bsz, seq, d)
    xp = _b_prompt(u3, xp.reshape(bsz, seq, d), w_dw, b_dw, ln_g, ln_b, w2, b_pw2).reshape(bsz * seq, d)
    conv_p = u3[:, seq - nw:]
    xs, ns_t = _b_sample(jnp.transpose(state, (1, 0, 2)), u_s, xs, w_dw, b_dw, ln_g, ln_b, w2, b_pw2)
    return xp, xs, conv_p, jnp.transpose(ns_t, (1, 0, 2))


C_QW = 128
C_DOWN = C_Q_LORA + C_KV_LORA


def _c_prep_body(h_ref, ct_ref, st_ref, gq_ref, gkv_ref, wqa_ref, wqb_ref, wuk_ref, wuv_ref,
                 q_ref, k_ref, v_ref, lat_ref, kr_ref):
    h = h_ref[...]
    ct = ct_ref[...]
    st = st_ref[...]
    cq = _rms(h[:, :C_Q_LORA], gq_ref[...]).astype(BF16)
    lat = _rms(h[:, C_Q_LORA:C_DOWN], gkv_ref[...])
    lat_ref[...] = lat
    lat_b = lat.astype(BF16)
    krp = h[:, C_DOWN:C_DOWN + C_QW] * ct + h[:, C_DOWN + C_QW:C_DOWN + 2 * C_QW] * st
    kr_ref[...] = krp[:, C_NOPE:C_NOPE + C_ROPE]
    qa = _nn(cq, wqa_ref[...])
    qb = _nn(cq, wqb_ref[...])
    kn = _nn(lat_b, wuk_ref[...])
    for hd in range(C_HEADS):
        sl = slice(hd * C_QW, (hd + 1) * C_QW)
        q_ref[:, sl] = (qa[:, sl] * ct + qb[:, sl] * st).astype(BF16)
        k_ref[:, sl] = (kn[:, sl] + krp).astype(BF16)
    v_ref[...] = _nn(lat_b, wuv_ref[...]).astype(BF16)


def _c_prep(h, ct, st, gq, gkv, wqa, wqb, wuk, wuv, *, tm=512):
    n, hc = h.shape
    tm = min(tm, n)
    nt = ct.shape[0] // tm
    qw = C_HEADS * C_QW
    vw = C_HEADS * C_V
    full = lambda a: pl.BlockSpec(a.shape, lambda i: (0, 0))
    return pl.pallas_call(
        _c_prep_body,
        grid=(n // tm,),
        in_specs=[pl.BlockSpec((tm, hc), lambda i: (i, 0)),
                  pl.BlockSpec((tm, C_QW), lambda i: (i % nt, 0)),
                  pl.BlockSpec((tm, C_QW), lambda i: (i % nt, 0)),
                  full(gq), full(gkv), full(wqa), full(wqb), full(wuk), full(wuv)],
        out_specs=[pl.BlockSpec((tm, qw), lambda i: (i, 0)),
                   pl.BlockSpec((tm, qw), lambda i: (i, 0)),
                   pl.BlockSpec((tm, vw), lambda i: (i, 0)),
                   pl.BlockSpec((tm, C_KV_LORA), lambda i: (i, 0)),
                   pl.BlockSpec((tm, C_ROPE), lambda i: (i, 0))],
        out_shape=[jax.ShapeDtypeStruct((n, qw), BF16),
                   jax.ShapeDtypeStruct((n, qw), BF16),
                   jax.ShapeDtypeStruct((n, vw), BF16),
                   jax.ShapeDtypeStruct((n, C_KV_LORA), F32),
                   jax.ShapeDtypeStruct((n, C_ROPE), F32)],
        compiler_params=_cp("parallel"),
        name="c_prep",
    )(h, ct, st, gq, gkv, wqa, wqb, wuk, wuv)


def _c_flash_body(q_ref, k_ref, vt_ref, ot_ref, m_ref, l_ref, acc_ref, s_ref, p_ref, a_ref, *, tq, scale):
    i = pl.program_id(2)
    t = C_KC
    m_ref[...] = jnp.full(m_ref.shape, NEG, F32)
    l_ref[...] = jnp.zeros(l_ref.shape, F32)
    acc_ref[...] = jnp.zeros(acc_ref.shape, F32)
    q = q_ref[0]
    krow = lax.broadcasted_iota(I32, (t, tq), 0)
    qcol = lax.broadcasted_iota(I32, (t, tq), 1)
    n_full = i * (tq // t)
    scale2 = scale * math.log2(math.e)

    def chunk(c, masked):
        off = pl.multiple_of(c * t, t)
        kc = k_ref[0, pl.ds(off, t), :]
        vtc = vt_ref[0, :, pl.ds(off, t)]
        for hh in range(C_GRP):
            s_ref[hh] = _nt(kc[:, hh * C_QW:(hh + 1) * C_QW], q[:, hh * C_QW:(hh + 1) * C_QW])
        for hh in range(C_GRP):
            s = s_ref[hh] * scale2
            if masked:
                s = jnp.where(off + krow <= i * tq + qcol, s, NEG)
            s_ref[hh] = s
            m_old = m_ref[hh]
            m_new = jnp.maximum(m_old, jnp.max(s, axis=0, keepdims=True))
            a_ref[hh] = jnp.exp2(m_old - m_new)
            m_ref[hh] = m_new
        for hh in range(C_GRP):
            p = jnp.exp2(s_ref[hh] - m_ref[hh])
            l_ref[hh] = a_ref[hh] * l_ref[hh] + jnp.sum(p, axis=0, keepdims=True)
            p_ref[hh] = p.astype(BF16)
        for hh in range(C_GRP):
            rows = slice(hh * C_V, (hh + 1) * C_V)
            acc_ref[rows, :] = a_ref[hh] * acc_ref[rows, :] + _nn(vtc[rows, :], p_ref[hh])

    def full_chunk(c, carry):
        chunk(c, False)
        return carry

    lax.fori_loop(0, n_full, full_chunk, 0)
    for dc in range(tq // t):
        chunk(n_full + dc, True)
    for hh in range(C_GRP):
        rows = slice(hh * C_V, (hh + 1) * C_V)
        ot_ref[0, rows, :] = (acc_ref[rows, :] / l_ref[hh]).astype(ot_ref.dtype)


C_GRP = 4
C_KC = 256


def _c_flash(q3, k3, vt3, *, tq=512):
    bsz, s, _ = q3.shape
    tq = min(tq, s)
    assert tq % C_KC == 0
    ngrp = C_HEADS // C_GRP
    scale = (C_NOPE + C_ROPE) ** -0.5
    body = functools.partial(_c_flash_body, tq=tq, scale=scale)
    return pl.pallas_call(
        body,
        grid=(bsz, ngrp, s // tq),
        in_specs=[pl.BlockSpec((1, tq, C_GRP * C_QW), lambda b, hp, i: (b, i, hp)),
                  pl.BlockSpec((1, s, C_GRP * C_QW), lambda b, hp, i: (b, 0, hp)),
                  pl.BlockSpec((1, C_GRP * C_V, s), lambda b, hp, i: (b, hp, 0))],
        out_specs=pl.BlockSpec((1, C_GRP * C_V, tq), lambda b, hp, i: (b, hp, i)),
        out_shape=jax.ShapeDtypeStruct((bsz, C_HEADS * C_V, s), BF16),
        scratch_shapes=[pltpu.VMEM((C_GRP, 1, tq), F32), pltpu.VMEM((C_GRP, 1, tq), F32),
                        pltpu.VMEM((C_GRP * C_V, tq), F32),
                        pltpu.VMEM((C_GRP, C_KC, tq), F32), pltpu.VMEM((C_GRP, C_KC, tq), BF16),
                        pltpu.VMEM((C_GRP, 1, tq), F32)],
        compiler_params=_cp("parallel", "parallel", "arbitrary"),
        name="c_flash",
    )(q3, k3, vt3)


def _head_mm_body(x_ref, w_ref, o_ref):
    o_ref[...] = _nn(x_ref[...], w_ref[0]).astype(o_ref.dtype)


def _head_mm(x, w3, *, out_dtype=F32):
    n = x.shape[0]
    nh, kin, kout = w3.shape
    return pl.pallas_call(
        _head_mm_body,
        grid=(nh,),
        in_specs=[pl.BlockSpec((n, kin), lambda h: (0, h)),
                  pl.BlockSpec((1, kin, kout), lambda h: (h, 0, 0))],
        out_specs=pl.BlockSpec((n, kout), lambda h: (0, h)),
        out_shape=jax.ShapeDtypeStruct((n, nh * kout), out_dtype),
        compiler_params=_cp("parallel"),
        name="head_mm",
    )(x, w3)


C_PG = 16


def _c_decode_body(pt_ref, *refs, scale):
    lat_refs = refs[:C_PG]
    rope_refs = refs[C_PG:2 * C_PG]
    ql_ref, qr_ref, latn_ref, krn_ref, o_ref, m_ref, l_ref, acc_ref = refs[2 * C_PG:]
    j = pl.program_id(1)

    @pl.when(j == 0)
    def _():
        m_ref[...] = jnp.full(m_ref.shape, NEG, F32)
        l_ref[...] = jnp.zeros(l_ref.shape, F32)
        acc_ref[...] = jnp.zeros(acc_ref.shape, F32)

    ql = ql_ref[0]
    qr = qr_ref[0]
    lat = jnp.concatenate([r[0] for r in lat_refs], axis=0).astype(BF16)
    s_rope = jnp.concatenate([_nn(qr, r[0].astype(BF16)) for r in rope_refs], axis=1)
    s = (_nt(ql, lat) + s_rope) * scale
    m_old = m_ref[...]
    m_new = jnp.maximum(m_old, jnp.max(s, axis=-1, keepdims=True))
    alpha = jnp.exp(m_old - m_new)
    p = jnp.exp(s - m_new)
    l_ref[...] = alpha * l_ref[...] + jnp.sum(p, axis=-1, keepdims=True)
    acc_ref[...] = alpha * acc_ref[...] + _nn(p.astype(BF16), lat)
    m_ref[...] = m_new

    @pl.when(j == pl.num_programs(1) - 1)
    def _():
        latn = latn_ref[0].astype(BF16)
        krn = krn_ref[0].astype(BF16)
        s_self = (jnp.sum(ql.astype(F32) * latn.astype(F32), axis=1, keepdims=True)
                  + jnp.sum(qr.astype(F32) * krn.astype(F32), axis=1, keepdims=True)) * scale
        m_o = m_ref[...]
        m_n = jnp.maximum(m_o, s_self)
        a = jnp.exp(m_o - m_n)
        p_s = jnp.exp(s_self - m_n)
        l = a * l_ref[...] + p_s
        acc = a * acc_ref[...] + p_s.astype(BF16).astype(F32) * latn.astype(F32)
        o_ref[0] = (acc / l).astype(o_ref.dtype)


def _c_decode(page_flat, n_pages, lat_cache, rope_cache_t, ql3, qr3, latn3, krn3):
    nb = ql3.shape[0]
    scale = (C_NOPE + C_ROPE) ** -0.5
    steps = n_pages // C_PG

    def lat_spec(i):
        return pl.BlockSpec((1, PAGE, C_KV_LORA), lambda b, j, pt: (pt[b * n_pages + j * C_PG + i], 0, 0))

    def rope_spec(i):
        return pl.BlockSpec((1, C_ROPE, PAGE), lambda b, j, pt: (pt[b * n_pages + j * C_PG + i], 0, 0))

    per = lambda shp: pl.BlockSpec((1,) + shp, lambda b, j, pt: (b, 0, 0))
    gs = pltpu.PrefetchScalarGridSpec(
        num_scalar_prefetch=1,
        grid=(nb, steps),
        in_specs=[lat_spec(i) for i in range(C_PG)] + [rope_spec(i) for i in range(C_PG)]
        + [per((C_HEADS, C_KV_LORA)), per((C_HEADS, C_ROPE)), per((1, C_KV_LORA)), per((1, C_ROPE))],
        out_specs=per((C_HEADS, C_KV_LORA)),
        scratch_shapes=[pltpu.VMEM((C_HEADS, 1), F32), pltpu.VMEM((C_HEADS, 1), F32),
                        pltpu.VMEM((C_HEADS, C_KV_LORA), F32)],
    )
    return pl.pallas_call(
        functools.partial(_c_decode_body, scale=scale),
        grid_spec=gs,
        out_shape=jax.ShapeDtypeStruct((nb, C_HEADS, C_KV_LORA), BF16),
        compiler_params=_cp("parallel", "arbitrary"),
        name="c_decode",
    )(page_flat, *([lat_cache] * C_PG), *([rope_cache_t] * C_PG), ql3, qr3, latn3, krn3)


def _rope_tables(pos):
    inv = ROPE_THETA ** (-jnp.arange(0, C_ROPE, 2, dtype=F32) / C_ROPE)
    ang = pos.astype(F32)[:, None] * inv[None, :]
    cos, sin = jnp.cos(ang), jnp.sin(ang)
    n = pos.shape[0]
    ones = jnp.ones((n, C_NOPE), F32)
    zeros = jnp.zeros((n, C_NOPE), F32)
    tail = jnp.zeros((n, C_QW - C_NOPE - C_ROPE), F32)
    ct = jnp.concatenate([ones, cos, cos, tail], axis=1)
    st = jnp.concatenate([zeros, -sin, sin, tail], axis=1)
    return ct, st


def _rope_swap(w):
    half = C_ROPE // 2
    return jnp.concatenate([w[..., half:], w[..., :half]], axis=-1)


def _pad_cols(w, left, total):
    return jnp.pad(w, ((0, 0), (left, total - left - w.shape[1])))


def _mixer_c(xp, xs, bsz, seq, lat_cache, rope_cache, page_table, g,
             c_w_down, c_g_q, c_g_kv, c_w_uq, c_w_uk, c_w_uv, c_w_out):
    nb = xs.shape[0]
    n_pages = page_table.shape[1]
    past = n_pages * PAGE
    w_rope = c_w_down[:, C_DOWN:]
    w_down = jnp.concatenate([c_w_down[:, :C_DOWN], _pad_cols(w_rope, C_NOPE, C_QW),
                              _pad_cols(_rope_swap(w_rope), C_NOPE, C_QW)], axis=1).astype(BF16)
    uq = c_w_uq.reshape(C_Q_LORA, C_HEADS, C_NOPE + C_ROPE)
    zq = jnp.zeros((C_Q_LORA, C_HEADS, C_QW - C_NOPE - C_ROPE), F32)
    wqa = jnp.concatenate([uq, zq], axis=2).reshape(C_Q_LORA, C_HEADS * C_QW).astype(BF16)
    wqb = jnp.concatenate([jnp.zeros((C_Q_LORA, C_HEADS, C_NOPE), F32), _rope_swap(uq[..., C_NOPE:]), zq],
                          axis=2).reshape(C_Q_LORA, C_HEADS * C_QW).astype(BF16)
    wuk = jnp.concatenate([c_w_uk, jnp.zeros((C_KV_LORA, C_HEADS, C_QW - C_NOPE), F32)],
                          axis=2).reshape(C_KV_LORA, C_HEADS * C_QW).astype(BF16)
    wuv = c_w_uv.reshape(C_KV_LORA, C_HEADS * C_V).astype(BF16)
    w_out = c_w_out.astype(BF16)
    gq = c_g_q.reshape(1, -1)
    gkv = c_g_kv.reshape(1, -1)

    h_p = _mm(xp, w_down, g=g)
    ct_p, st_p = _rope_tables(jnp.arange(seq, dtype=I32))
    q, k, v, lat_p, kr_p = _c_prep(h_p, ct_p, st_p, gq, gkv, wqa, wqb, wuk, wuv)
    vt = jnp.transpose(v.reshape(bsz, seq, -1), (0, 2, 1))
    ot = _c_flash(q.reshape(bsz, seq, -1), k.reshape(bsz, seq, -1), vt)
    xp = _mm(jnp.transpose(ot, (0, 2, 1)).reshape(bsz * seq, -1), w_out, res=xp)

    h_s = _mm(xs, w_down, g=g)
    ct_s, st_s = _rope_tables(jnp.full((nb,), past, I32))
    q_s, _, _, lat_s, kr_s = _c_prep(h_s, ct_s, st_s, gq, gkv, wqa, wqb, wuk, wuv)
    wukt = jnp.concatenate([jnp.transpose(c_w_uk, (1, 2, 0)),
                            jnp.zeros((C_HEADS, C_QW - C_NOPE, C_KV_LORA), F32)], axis=1).astype(BF16)
    ql = _head_mm(q_s, wukt, out_dtype=BF16).reshape(nb, C_HEADS, C_KV_LORA)
    qr = q_s.reshape(nb, C_HEADS, C_QW)[:, :, C_NOPE:C_NOPE + C_ROPE]
    o_lat = _c_decode(page_table.reshape(-1), n_pages, lat_cache, jnp.transpose(rope_cache, (0, 2, 1)),
                      ql, qr, lat_s.reshape(nb, 1, -1), kr_s.reshape(nb, 1, -1))
    wuv2 = jnp.transpose(c_w_uv, (1, 0, 2)).reshape(C_HEADS // 2, 2, C_KV_LORA, C_V)
    z = jnp.zeros_like(wuv2[:, 0])
    wuv_pair = jnp.concatenate([jnp.concatenate([wuv2[:, 0], z], axis=2),
                                jnp.concatenate([z, wuv2[:, 1]], axis=2)], axis=1).astype(BF16)
    o_s = _head_mm(o_lat.reshape(nb, C_HEADS * C_KV_LORA), wuv_pair, out_dtype=BF16)
    xs = _mm(o_s, w_out, res=xs)
    return (xp, xs, lat_p.reshape(bsz, seq, -1), kr_p.reshape(bsz, seq, -1),
            lat_s.reshape(nb, 1, -1), kr_s.reshape(nb, 1, -1))


D_Q0 = 0
D_KV0 = D_HEADS * D_HD
D_QI0 = D_KV0 + 2 * D_KV_HEADS * D_HD
D_KI0 = D_QI0 + D_IDX_HEADS * D_IDX_DIM
D_WI0 = D_KI0 + D_IDX_DIM
D_HW = D_WI0 + LANES - D_IDX_DIM
INT_MIN = -2 ** 31


def _sort_key(score):
    bits = pltpu.bitcast(score, I32)
    return jnp.where(bits >= 0, bits, bits ^ jnp.int32(0x7FFFFFFF))


CNT_UNROLL = 4
D_TQ = 256


def _kth_largest(key_ref, n_groups, k):
    nq = key_ref.shape[1]
    kf = jnp.float32(k)

    def count(pred_fn):
        def body(gi, acc):
            for u in range(CNT_UNROLL):
                off = pl.multiple_of((gi * CNT_UNROLL + u) * LANES, LANES)
                hit = jnp.where(pred_fn(key_ref[pl.ds(off, LANES), :], off), 1.0, 0.0)
                acc = acc + jnp.sum(hit.reshape(LANES // 8, 8, nq), axis=0)
            return acc
        part = lax.fori_loop(0, n_groups, body, jnp.zeros((8, nq), F32))
        return jnp.sum(part, axis=0, keepdims=True)

    nonneg = count(lambda x, off: x >= 0) >= kf
    prefix = jnp.where(nonneg, jnp.int32(0), jnp.int32(INT_MIN))

    def bit_body(i, prefix):
        cand = prefix | lax.shift_left(jnp.int32(1), 30 - i)
        return jnp.where(count(lambda x, off: x >= cand) >= kf, cand, prefix)

    thr = lax.fori_loop(0, 31, bit_body, prefix)
    n_gt = count(lambda x, off: x > thr)
    n_eq = count(lambda x, off: x == thr)
    need = kf - n_gt

    def tie_search(_):
        row = lax.broadcasted_iota(I32, (LANES, nq), 0)

        def body(i, lo):
            cand = lo + lax.shift_right_logical(jnp.int32(1 << 29), i)
            c = count(lambda x, off: (x == thr) & (row + off < cand))
            return jnp.where(c < need, cand, lo)
        return lax.fori_loop(0, 30, body, jnp.zeros((1, nq), I32)) + 1

    any_tie = jnp.max(jnp.where(n_eq > need, 1.0, 0.0)) > 0.0
    bound = lax.cond(any_tie, tie_search, lambda _: jnp.full((1, nq), 2 ** 30, I32), 0)
    return thr, bound


def _d_prompt_body(q_ref, qi_ref, wit_ref, k_ref, vt_ref, ki_ref, bank_ref, ot_ref,
                   key_ref, m_ref, l_ref, acc_ref, s_ref, p_ref, a_ref, *, k_top, n_bank):
    t = Q_TILE
    tq = D_TQ
    nsub = tq // t
    blk = pl.program_id(1)
    n_chunks = (blk + 1) * nsub
    krow = lax.broadcasted_iota(I32, (t, tq), 0)
    qpos = blk * tq + lax.broadcasted_iota(I32, (t, tq), 1)
    qi = qi_ref[0]
    wit = wit_ref[0] * (D_IDX_HEADS ** -0.5) * (D_IDX_DIM ** -0.5)

    def score_chunk(c, carry):
        off = pl.multiple_of(c * t, t)
        ki = ki_ref[0, pl.ds(off, t), :][:, :D_IDX_DIM].astype(BF16)
        dots = _nt(ki, qi)
        sc = jnp.zeros((t, tq), F32)
        for h in range(D_IDX_HEADS):
            sc = sc + wit[h:h + 1, :] * jnp.maximum(dots[:, h * tq:(h + 1) * tq], 0.0)
        key = _sort_key(sc)
        key_ref[pl.ds(off, t), :] = jnp.where(off + krow <= qpos, key, jnp.int32(INT_MIN))
        return carry

    lax.fori_loop(0, n_chunks, score_chunk, 0)
    n_groups = (n_chunks + CNT_UNROLL - 1) // CNT_UNROLL

    def pad_chunk(c, carry):
        key_ref[pl.ds(pl.multiple_of(c * t, t), t), :] = jnp.full((t, tq), INT_MIN, I32)
        return carry

    lax.fori_loop(n_chunks, n_groups * CNT_UNROLL, pad_chunk, 0)
    thr, bound = _kth_largest(key_ref, n_groups, k_top)
    thr = jnp.maximum(thr, jnp.int32(INT_MIN + 1))

    m_ref[...] = jnp.full(m_ref.shape, M_INIT, F32)
    l_ref[...] = jnp.zeros(l_ref.shape, F32)
    acc_ref[...] = jnp.zeros(acc_ref.shape, F32)
    q = (q_ref[0] * (D_HD ** -0.5)).astype(BF16)
    rep = D_HEADS // D_KV_HEADS

    def attend_chunk(c, carry):
        off = pl.multiple_of(c * t, t)
        key = key_ref[pl.ds(off, t), :]
        sel = (key > thr) | ((key == thr) & (krow + off < bound))
        kc = k_ref[0, pl.ds(off, t), :]
        vtc = vt_ref[0, :, pl.ds(off, t)]
        ds_ = [jnp.clip(blk * nsub + u - c, 0, n_bank - 1) for u in range(nsub)]
        for h in range(D_HEADS):
            gk = h // rep
            s_ref[h] = _nt(kc[:, gk * D_HD:(gk + 1) * D_HD], q[:, h * D_HD:(h + 1) * D_HD])
        for h in range(D_HEADS):
            bias = jnp.concatenate([bank_ref[du, h] for du in ds_], axis=1)
            s = jnp.where(sel, s_ref[h] + bias, NEG)
            s_ref[h] = s
            m_old = m_ref[h]
            m_new = jnp.maximum(m_old, jnp.max(s, axis=0, keepdims=True))
            a_ref[h] = jnp.exp(m_old - m_new)
            m_ref[h] = m_new
        for h in range(D_HEADS):
            p = jnp.exp(s_ref[h] - m_ref[h])
            l_ref[h] = a_ref[h] * l_ref[h] + jnp.sum(p, axis=0, keepdims=True)
            p_ref[h] = p.astype(BF16)
        for h in range(D_HEADS):
            gk = h // rep
            acc_ref[h] = a_ref[h] * acc_ref[h] + _nn(vtc[gk * D_HD:(gk + 1) * D_HD, :], p_ref[h])
        return carry

    lax.fori_loop(0, n_chunks, attend_chunk, 0)
    for h in range(D_HEADS):
        ot_ref[0, h * D_HD:(h + 1) * D_HD, :] = (acc_ref[h] / l_ref[h]).astype(ot_ref.dtype)


def _d_prompt(h3, qir, wit, kb, vt, bank, k_top):
    bsz, seq, _ = h3.shape
    t = Q_TILE
    tq = D_TQ
    assert seq % (CNT_UNROLL * t) == 0 and seq % tq == 0
    n_bank = bank.shape[0]
    body = functools.partial(_d_prompt_body, k_top=k_top, n_bank=n_bank)
    qw = D_HEADS * D_HD
    kw = D_KV_HEADS * D_HD
    return pl.pallas_call(
        body,
        grid=(bsz, seq // tq),
        in_specs=[pl.BlockSpec((1, tq, qw), lambda b, i: (b, i, 0)),
                  pl.BlockSpec((1, D_IDX_HEADS * tq, D_IDX_DIM), lambda b, i: (b, i, 0)),
                  pl.BlockSpec((1, D_IDX_HEADS, tq), lambda b, i: (b, 0, i)),
                  pl.BlockSpec((1, seq, kw), lambda b, i: (b, 0, 0)),
                  pl.BlockSpec((1, kw, seq), lambda b, i: (b, 0, 0)),
                  pl.BlockSpec((1, seq, LANES), lambda b, i: (b, 0, D_KI0 // LANES)),
                  pl.BlockSpec(bank.shape, lambda b, i: (0, 0, 0, 0), pipeline_mode=pl.Buffered(1))],
        out_specs=pl.BlockSpec((1, qw, tq), lambda b, i: (b, 0, i)),
        out_shape=jax.ShapeDtypeStruct((bsz, qw, seq), BF16),
        scratch_shapes=[pltpu.VMEM((seq, tq), I32),
                        pltpu.VMEM((D_HEADS, 1, tq), F32), pltpu.VMEM((D_HEADS, 1, tq), F32),
                        pltpu.VMEM((D_HEADS, D_HD, tq), F32),
                        pltpu.VMEM((D_HEADS, t, tq), F32), pltpu.VMEM((D_HEADS, t, tq), BF16),
                        pltpu.VMEM((D_HEADS, 1, tq), F32)],
        compiler_params=_cp("parallel", "arbitrary"),
        name="d_prompt",
    )(h3, qir, wit, kb, vt, h3, bank)


D_PG = 16


def _d_index_body(pt_ref, *refs):
    page_refs = refs[:D_PG]
    qi_ref, wi_ref, kin_ref, sc_ref, self_ref = refs[D_PG:]
    j = pl.program_id(1)
    qi = qi_ref[0]
    wi = wi_ref[0] * (D_IDX_HEADS ** -0.5)
    qb = qi.astype(BF16)
    parts = []
    for r in page_refs:
        dots = _nn(qb, r[0].astype(BF16)) * (D_IDX_DIM ** -0.5)
        parts.append(jnp.sum(wi * jnp.maximum(dots, 0.0), axis=0, keepdims=True))
    sc_ref[0] = jnp.concatenate(parts, axis=1)

    @pl.when(j == pl.num_programs(1) - 1)
    def _():
        kin = kin_ref[0].astype(BF16).astype(F32)
        dots = jnp.sum(qb.astype(F32) * kin, axis=1, keepdims=True) * (D_IDX_DIM ** -0.5)
        own = jnp.sum(wi * jnp.maximum(dots, 0.0), axis=0, keepdims=True)
        lane = lax.broadcasted_iota(I32, (1, LANES), 1)
        self_ref[0] = jnp.where(lane == 0, own, -jnp.inf)


def _d_index(page_flat, n_pages, kidx_t, qi3, wi3, kin3):
    nb = qi3.shape[0]
    steps = n_pages // D_PG

    def page_spec(i):
        return pl.BlockSpec((1, D_IDX_DIM, PAGE), lambda b, j, pt: (pt[b * n_pages + j * D_PG + i], 0, 0))

    per = lambda shp: pl.BlockSpec((1,) + shp, lambda b, j, pt: (b, 0, 0))
    gs = pltpu.PrefetchScalarGridSpec(
        num_scalar_prefetch=1,
        grid=(nb, steps),
        in_specs=[page_spec(i) for i in range(D_PG)]
        + [per((D_IDX_HEADS, D_IDX_DIM)), per((D_IDX_HEADS, 1)), per((1, D_IDX_DIM))],
        out_specs=[pl.BlockSpec((1, 1, D_PG * PAGE), lambda b, j, pt: (b, 0, j)),
                   per((1, LANES))],
    )
    return pl.pallas_call(
        _d_index_body,
        grid_spec=gs,
        out_shape=[jax.ShapeDtypeStruct((nb, 1, n_pages * PAGE), F32),
                   jax.ShapeDtypeStruct((nb, 1, LANES), F32)],
        compiler_params=_cp("parallel", "arbitrary"),
        name="d_index",
    )(page_flat, *([kidx_t] * D_PG), qi3, wi3, kin3)


def _d_thr_body(sc_ref, thr_ref, bound_ref, key_ref, *, k_top):
    n_chunks = sc_ref.shape[0] // LANES

    def fill(c, carry):
        off = pl.multiple_of(c * LANES, LANES)
        key_ref[pl.ds(off, LANES), :] = _sort_key(sc_ref[pl.ds(off, LANES), :])
        return carry

    lax.fori_loop(0, n_chunks, fill, 0)
    thr, bound = _kth_largest(key_ref, n_chunks // CNT_UNROLL, k_top)
    thr_ref[...] = thr
    bound_ref[...] = bound


def _d_thr(scores_t, k_top):
    nk, nq = scores_t.shape
    return pl.pallas_call(
        functools.partial(_d_thr_body, k_top=k_top),
        grid=(1,),
        in_specs=[pl.BlockSpec((nk, nq), lambda i: (0, 0))],
        out_specs=[pl.BlockSpec((1, nq), lambda i: (0, 0))] * 2,
        out_shape=[jax.ShapeDtypeStruct((1, nq), I32)] * 2,
        scratch_shapes=[pltpu.VMEM((nk, nq), I32)],
        compiler_params=_cp("arbitrary"),
        name="d_thr",
    )(scores_t)


def _d_decode_body(pt_ref, thr_ref, bnd_ref, *refs):
    kv_refs = refs[:D_PG]
    sc_ref, self_ref, q_ref, kvn_ref, g_ref, o_ref, m_ref, l_ref, acc_ref = refs[D_PG:]
    b = pl.program_id(0)
    j = pl.program_id(1)
    thr = thr_ref[b]
    bound = bnd_ref[b]
    kw = D_KV_HEADS * D_HD
    rep = D_HEADS // D_KV_HEADS
    hrow = lax.broadcasted_iota(I32, (D_HEADS, kw), 0)
    hcol = lax.broadcasted_iota(I32, (D_HEADS, kw), 1)
    diag = (hcol // D_HD) == (hrow // rep)

    @pl.when(j == 0)
    def _():
        m_ref[...] = jnp.full(m_ref.shape, NEG, F32)
        l_ref[...] = jnp.zeros(l_ref.shape, F32)
        acc_ref[...] = jnp.zeros(acc_ref.shape, F32)

    q = q_ref[0]
    qbd = jnp.where(diag, jnp.concatenate([q] * D_KV_HEADS, axis=1), 0.0).astype(BF16)
    kt = jnp.concatenate([r[0, 0] for r in kv_refs], axis=1).astype(BF16)
    vt = jnp.concatenate([r[0, 1] for r in kv_refs], axis=1).astype(BF16)
    width = D_PG * PAGE
    off = pl.multiple_of(j * width, width)
    key = _sort_key(sc_ref[0])
    pos = off + lax.broadcasted_iota(I32, (1, width), 1)
    sel = (key > thr) | ((key == thr) & (pos < bound))
    s = _nn(qbd, kt) * (D_HD ** -0.5) + g_ref[:, pl.ds(off, width)]
    s = jnp.where(sel, s, NEG)
    m_old = m_ref[...]
    m_new = jnp.maximum(m_old, jnp.max(s, axis=-1, keepdims=True))
    alpha = jnp.exp(m_old - m_new)
    p = jnp.where(sel, jnp.exp(s - m_new), 0.0)
    l_ref[...] = alpha * l_ref[...] + jnp.sum(p, axis=-1, keepdims=True)
    acc_ref[...] = alpha * acc_ref[...] + _nt(p.astype(BF16), vt)
    m_ref[...] = m_new

    @pl.when(j == pl.num_programs(1) - 1)
    def _():
        past = pl.num_programs(1) * width
        kn = kvn_ref[0, :, :kw].astype(BF16).astype(F32)
        vn = kvn_ref[0, :, kw:].astype(BF16).astype(F32)
        qf = qbd.astype(F32)
        s_self = (jnp.sum(qf * kn, axis=1, keepdims=True) * (D_HD ** -0.5)
                  + g_ref[:, pl.ds(pl.multiple_of(past, LANES), LANES)][:, 0:1])
        key_s = _sort_key(self_ref[0][:, 0:1])
        sel_s = (key_s > thr) | ((key_s == thr) & (past < bound))
        s_self = jnp.where(sel_s, s_self, NEG)
        m_o = m_ref[...]
        m_n = jnp.maximum(m_o, s_self)
        a = jnp.exp(m_o - m_n)
        p_s = jnp.where(sel_s, jnp.exp(s_self - m_n), 0.0)
        l = a * l_ref[...] + p_s
        acc = a * acc_ref[...] + p_s.astype(BF16).astype(F32) * vn
        o_full = jnp.where(diag, acc / l, 0.0)
        o_ref[0] = (o_full[:, 0:D_HD] + o_full[:, D_HD:2 * D_HD]
                    + o_full[:, 2 * D_HD:3 * D_HD] + o_full[:, 3 * D_HD:]).astype(o_ref.dtype)


def _d_decode(page_flat, thr, bound, n_pages, kv_t, scores3, self3, q3, kvn3, gtab):
    nb = q3.shape[0]
    steps = n_pages // D_PG
    kw = D_KV_HEADS * D_HD

    def kv_spec(i):
        return pl.BlockSpec((1, 2, kw, PAGE),
                            lambda b, j, pt, th, bd: (pt[b * n_pages + j * D_PG + i], 0, 0, 0))

    per = lambda shp: pl.BlockSpec((1,) + shp, lambda b, j, pt, th, bd: (b, 0, 0))
    gs = pltpu.PrefetchScalarGridSpec(
        num_scalar_prefetch=3,
        grid=(nb, steps),
        in_specs=[kv_spec(i) for i in range(D_PG)]
        + [pl.BlockSpec((1, 1, D_PG * PAGE), lambda b, j, pt, th, bd: (b, 0, j)),
           per((1, LANES)), per((D_HEADS, D_HD)), per((1, 2 * kw)),
           pl.BlockSpec(gtab.shape, lambda b, j, pt, th, bd: (0, 0))],
        out_specs=per((D_HEADS, D_HD)),
        scratch_shapes=[pltpu.VMEM((D_HEADS, 1), F32), pltpu.VMEM((D_HEADS, 1), F32),
                        pltpu.VMEM((D_HEADS, kw), F32)],
    )
    return pl.pallas_call(
        _d_decode_body,
        grid_spec=gs,
        out_shape=jax.ShapeDtypeStruct((nb, D_HEADS, D_HD), BF16),
        compiler_params=_cp("parallel", "arbitrary"),
        name="d_decode",
    )(page_flat, thr, bound, *([kv_t] * D_PG), scores3, self3, q3, kvn3, gtab)


def _d_bias_by_dist(rel_bias, n):
    return rel_bias[:, :D_HEADS][_rel_bucket(jnp.arange(n, dtype=I32))].astype(F32).T


def _mixer_d(xp, xs, bsz, seq, kv_cache, kidx_cache, page_table, rel_bias, g, d_w_in, d_w_out):
    nb = xs.shape[0]
    n_pages = page_table.shape[1]
    past = n_pages * PAGE
    w_in = jnp.pad(d_w_in, ((0, 0), (0, D_HW - d_w_in.shape[1]))).astype(BF16)
    w_out = d_w_out.astype(BF16)
    kw = D_KV_HEADS * D_HD

    h_p = _mm(xp, w_in, g=g)
    h3 = h_p.reshape(bsz, seq, D_HW)
    t = Q_TILE
    n_bank = min(seq // t, -(-(REL_MAX_DIST + t) // t) + 1)
    vec = _d_bias_by_dist(rel_bias, n_bank * t)
    w = jnp.concatenate([vec, jnp.zeros((D_HEADS, t), F32)], axis=1)
    hank = _toeplitz(w, t, n_bank * t)
    bank = jnp.transpose(hank.reshape(D_HEADS, t, n_bank, t), (2, 0, 1, 3))
    wit = jnp.transpose(h3[:, :, D_WI0:D_WI0 + D_IDX_HEADS], (0, 2, 1))
    kb = h3[:, :, D_KV0:D_KV0 + kw].astype(BF16)
    vt = jnp.transpose(h3[:, :, D_KV0 + kw:D_QI0], (0, 2, 1)).astype(BF16)
    qir = jnp.transpose(h3[:, :, D_QI0:D_KI0].reshape(bsz, seq // D_TQ, D_TQ, D_IDX_HEADS, D_IDX_DIM),
                        (0, 1, 3, 2, 4)).reshape(bsz, seq * D_IDX_HEADS, D_IDX_DIM).astype(BF16)
    ot = _d_prompt(h3, qir, wit, kb, vt, bank, min(D_TOPK_MAX, seq // 4))
    xp = _mm(jnp.transpose(ot, (0, 2, 1)).reshape(bsz * seq, -1), w_out, res=xp)
    kv_p = h3[:, :, D_KV0:D_QI0].reshape(bsz, seq, 2, D_KV_HEADS, D_HD)
    kidx_p = h3[:, :, D_KI0:D_WI0]

    h_s = _mm(xs, w_in, g=g)
    page_flat = page_table.reshape(-1)
    qi3 = h_s[:, D_QI0:D_KI0].reshape(nb, D_IDX_HEADS, D_IDX_DIM)
    wi3 = h_s[:, D_WI0:D_WI0 + D_IDX_HEADS].reshape(nb, D_IDX_HEADS, 1)
    kin3 = h_s[:, D_KI0:D_WI0].reshape(nb, 1, D_IDX_DIM)
    sc3, self3 = _d_index(page_flat, n_pages, jnp.transpose(kidx_cache, (0, 2, 1)), qi3, wi3, kin3)
    n_rows = -(-(past + LANES) // (CNT_UNROLL * LANES)) * (CNT_UNROLL * LANES)
    scores_t = jnp.concatenate([sc3.reshape(nb, past), self3.reshape(nb, LANES),
                                jnp.full((nb, n_rows - past - LANES), -jnp.inf, F32)], axis=1).T
    thr, bound = _d_thr(scores_t, min(D_TOPK_MAX, (past + 1) // 4))
    vec_s = _d_bias_by_dist(rel_bias, past + 1)
    gtab_s = jnp.concatenate([vec_s[:, :0:-1], jnp.broadcast_to(vec_s[:, 0:1], (D_HEADS, LANES))], axis=1)
    kv_t = jnp.transpose(kv_cache, (0, 2, 3, 4, 1)).reshape(kv_cache.shape[0], 2, kw, PAGE)
    o_s = _d_decode(page_flat, thr[0], bound[0], n_pages, kv_t, sc3, self3,
                    h_s[:, :D_KV0].reshape(nb, D_HEADS, D_HD), h_s[:, D_KV0:D_QI0].reshape(nb, 1, 2 * kw), gtab_s)
    xs = _mm(o_s.reshape(nb, -1), w_out, res=xs)
    kv_s = h_s[:, D_KV0:D_QI0].reshape(nb, 1, 2, D_KV_HEADS, D_HD)
    kidx_s = h_s[:, D_KI0:D_WI0].reshape(nb, 1, D_IDX_DIM)
    return xp, xs, kv_p, kidx_p, kv_s, kidx_s


def kernel(x_prompt, x_sample, mem_prompt, cache_a1_kv, cache_a2_kv, cache_a3_kv, state_b_conv,
           cache_c_latent, cache_c_krope, cache_d_kv, cache_d_kidx, cache_mem_kv, page_table,
           rel_bias, g_mix, g_cross, g_ffn, g_final, w_xq, w_xkv, w_xo, w_ffn_in, w_ffn_out,
           a_w_in, a_w_out, b_w_pw1, b_b_pw1, b_w_dw, b_b_dw, b_ln_g, b_ln_b, b_w_pw2, b_b_pw2,
           c_w_down, c_g_q, c_g_kv, c_w_uq, c_w_uk, c_w_uv, c_w_out, d_w_in, d_w_out):
    bsz, seq, d = x_prompt.shape
    nb = x_sample.shape[0]
    assert x_sample.shape[1] == 1
    depth = g_mix.shape[0]
    xp = x_prompt.reshape(bsz * seq, d)
    xs = x_sample.reshape(nb, d)
    mem2 = mem_prompt.reshape(bsz * MEM_LEN, d)
    hw = X_HEADS * X_HD
    mem5 = cache_mem_kv.reshape(depth, nb, MEM_LEN, 2 * X_HEADS, X_HD)
    mem_kv_out = []
    outs = {}
    for i in range(depth):
        kind = i % 4
        if kind == 0:
            xp, xs, a_p, a_s = _mixer_a(xp, xs, bsz, seq, [cache_a1_kv, cache_a2_kv, cache_a3_kv],
                                        rel_bias, g_mix[i], a_w_in, a_w_out)
            outs["a_p"], outs["a_s"] = a_p, a_s
        elif kind == 1:
            xp, xs, conv_p, conv_s = _mixer_b(xp, xs, bsz, seq, state_b_conv, g_mix[i], b_w_pw1, b_b_pw1,
                                              b_w_dw, b_b_dw, b_ln_g, b_ln_b, b_w_pw2, b_b_pw2)
            outs["conv"] = (conv_p, conv_s)
        elif kind == 2:
            xp, xs, lat_p, kr_p, lat_s, kr_s = _mixer_c(xp, xs, bsz, seq, cache_c_latent, cache_c_krope,
                                                        page_table, g_mix[i], c_w_down, c_g_q, c_g_kv,
                                                        c_w_uq, c_w_uk, c_w_uv, c_w_out)
            outs["c"] = (lat_p, kr_p, lat_s, kr_s)
        else:
            xp, xs, kv_p, kidx_p, kv_s, kidx_s = _mixer_d(xp, xs, bsz, seq, cache_d_kv, cache_d_kidx,
                                                          page_table, rel_bias, g_mix[i], d_w_in, d_w_out)
            outs["d"] = (kv_p, kidx_p, kv_s, kidx_s)
        mkv = _mm(mem2, w_xkv[i].astype(BF16))
        mem_kv_out.append(mkv.reshape(bsz, MEM_LEN, 2, X_HEADS, X_HD))
        wq = w_xq[i].astype(BF16)
        wo = w_xo[i].astype(BF16)
        xp = _cross(xp.reshape(bsz, seq, d), g_cross[i], wq, wo, mkv.reshape(bsz, MEM_LEN, 2 * hw)).reshape(bsz * seq, d)
        q_s = _mm(xs, wq, g=g_cross[i]).reshape(nb, X_HEADS, X_HD)
        xs = _mm(_cross_s(q_s, mem5, i).reshape(nb, hw), wo, res=xs)
        w_in = w_ffn_in[i].astype(BF16)
        w_out = w_ffn_out[i].astype(BF16)
        xp = _ffn(xp, g_ffn[i], w_in, w_out)
        xs = _ffn(xs, g_ffn[i], w_in, w_out)
    y_p = _rmsnorm(xp, g_final).reshape(bsz, seq, d)
    y_s = _rmsnorm(xs, g_final).reshape(nb, 1, d)
    a_p, a_s = outs["a_p"], outs["a_s"]
    conv_p, conv_s = outs["conv"]
    lat_p, kr_p, lat_s, kr_s = outs["c"]
    kv_p, kidx_p, kv_s, kidx_s = outs["d"]
    return (y_p, y_s, a_p[0], a_p[1], a_p[2], a_s[0], a_s[1], a_s[2], conv_p, conv_s,
            lat_p, kr_p, lat_s, kr_s, kv_p, kidx_p, kv_s, kidx_s, jnp.stack(mem_kv_out))
```

```python
import functools
import math

import jax
import jax.numpy as jnp
import numpy as np
from jax import lax
from jax.experimental import pallas as pl
from jax.experimental.pallas import tpu as pltpu

F32 = jnp.float32
BF16 = jnp.bfloat16
I32 = jnp.int32

EPS = 1e-6
PAGE = 128
LANES = 128
VMEM_LIMIT = 52 * 1024 * 1024
NEG = -1e30
M_INIT = -5e29

REL_BUCKETS = 32
REL_MAX_DIST = 2048
A_WINDOWS = (128, 512, 2048)
A_DILATIONS = (1, 4, 16)
A_HEADS = 8
A_HD = 64
A_NKEYS = 129
B_CONV_WIDTH = 31
C_HEADS = 16
C_Q_LORA = 384
C_KV_LORA = 256
C_NOPE = 64
C_ROPE = 32
C_V = 64
ROPE_THETA = 10000.0
D_HEADS = 16
D_KV_HEADS = 4
D_HD = 64
D_IDX_HEADS = 8
D_IDX_DIM = 64
D_TOPK_MAX = 256
X_HEADS = 4
X_HD = 128
MEM_LEN = 256
Q_TILE = 128


def _cp(*sem):
    return pltpu.CompilerParams(dimension_semantics=sem, vmem_limit_bytes=VMEM_LIMIT)


def _nt(a, b):
    return lax.dot_general(a, b, (((1,), (1,)), ((), ())), preferred_element_type=F32)


def _nn(a, b):
    return jnp.dot(a, b, preferred_element_type=F32)


def _sigmoid(x):
    return 1.0 / (1.0 + jnp.exp(-x))


def _rms(x, g):
    return x * lax.rsqrt(jnp.mean(x * x, axis=-1, keepdims=True) + EPS) * g


def _rel_bucket(dist):
    n = jnp.maximum(dist, 0)
    max_exact = REL_BUCKETS // 2
    nf = jnp.maximum(n, 1).astype(F32)
    large = max_exact + (jnp.log(nf / max_exact) / math.log(REL_MAX_DIST / max_exact)
                         * (REL_BUCKETS - max_exact)).astype(I32)
    large = jnp.minimum(large, REL_BUCKETS - 1)
    return jnp.where(n < max_exact, n, large)


def _mm_body(*refs, norm, bias, glu, resid):
    it = iter(refs)
    x_ref = next(it)
    w_ref = next(it)
    g_ref = next(it) if norm else None
    b_ref = next(it) if bias else None
    w2_ref = next(it) if glu else None
    b2_ref = next(it) if (glu and bias) else None
    r_ref = next(it) if resid else None
    o_ref = next(it)
    xn_ref = next(it)

    @pl.when(pl.program_id(1) == 0)
    def _():
        x = x_ref[...].astype(F32)
        if norm:
            x = _rms(x, g_ref[...])
        xn_ref[...] = x.astype(BF16)

    xn = xn_ref[...]
    h = _nn(xn, w_ref[...])
    if bias:
        h = h + b_ref[...]
    if glu:
        h2 = _nn(xn, w2_ref[...])
        if bias:
            h2 = h2 + b2_ref[...]
        h = h * _sigmoid(h2)
    if resid:
        h = h + r_ref[...]
    o_ref[...] = h.astype(o_ref.dtype)


def _mm(x, w, *, g=None, b=None, w2=None, b2=None, res=None, out_dtype=F32, tm=512, tn=None):
    n, k = x.shape
    m = w.shape[1]
    tm = min(tm, n)
    tn = tn or m
    assert n % tm == 0 and m % tn == 0
    args = [x, w]
    specs = [pl.BlockSpec((tm, k), lambda i, j: (i, 0)),
             pl.BlockSpec((k, tn), lambda i, j: (0, j))]
    if g is not None:
        args.append(g.reshape(1, k))
        specs.append(pl.BlockSpec((1, k), lambda i, j: (0, 0)))
    if b is not None:
        args.append(b.reshape(1, m))
        specs.append(pl.BlockSpec((1, tn), lambda i, j: (0, j)))
    if w2 is not None:
        args.append(w2)
        specs.append(pl.BlockSpec((k, tn), lambda i, j: (0, j)))
        if b2 is not None:
            args.append(b2.reshape(1, m))
            specs.append(pl.BlockSpec((1, tn), lambda i, j: (0, j)))
    if res is not None:
        args.append(res)
        specs.append(pl.BlockSpec((tm, tn), lambda i, j: (i, j)))
    body = functools.partial(_mm_body, norm=g is not None, bias=b is not None,
                             glu=w2 is not None, resid=res is not None)
    return pl.pallas_call(
        body,
        grid=(n // tm, m // tn),
        in_specs=specs,
        out_specs=pl.BlockSpec((tm, tn), lambda i, j: (i, j)),
        out_shape=jax.ShapeDtypeStruct((n, m), out_dtype),
        scratch_shapes=[pltpu.VMEM((tm, k), BF16)],
        compiler_params=_cp("parallel", "arbitrary"),
        name="mm",
    )(*args)


def _ffn_body(x_ref, g_ref, wg_ref, wu_ref, wo_ref, o_ref, xn_ref):
    j = pl.program_id(1)

    @pl.when(j == 0)
    def _():
        x = x_ref[...]
        xn_ref[...] = _rms(x, g_ref[...]).astype(BF16)
        o_ref[...] = x

    xn = xn_ref[...]
    hg = _nn(xn, wg_ref[...])
    hu = _nn(xn, wu_ref[...])
    a = (hg * _sigmoid(hg) * hu).astype(BF16)
    o_ref[...] += _nn(a, wo_ref[...])


def _ffn(x, g, w_in, w_out, *, tm=512):
    n, d = x.shape
    dff = w_out.shape[0]
    tf = dff // 2
    assert tf % LANES == 0
    tm = min(tm, n)
    nf = dff // tf
    return pl.pallas_call(
        _ffn_body,
        grid=(n // tm, nf),
        in_specs=[pl.BlockSpec((tm, d), lambda i, j: (i, 0)),
                  pl.BlockSpec((1, d), lambda i, j: (0, 0)),
                  pl.BlockSpec((d, tf), lambda i, j: (0, j)),
                  pl.BlockSpec((d, tf), lambda i, j: (0, j + nf)),
                  pl.BlockSpec((tf, d), lambda i, j: (j, 0))],
        out_specs=pl.BlockSpec((tm, d), lambda i, j: (i, 0)),
        out_shape=jax.ShapeDtypeStruct((n, d), F32),
        scratch_shapes=[pltpu.VMEM((tm, d), BF16)],
        compiler_params=_cp("parallel", "arbitrary"),
        name="ffn",
    )(x, g.reshape(1, d), w_in, w_in, w_out)


def _cross_body(x_ref, g_ref, wq_ref, wo_ref, kv_ref, o_ref):
    x = x_ref[0]
    xn = _rms(x, g_ref[...]).astype(BF16)
    q = _nn(xn, wq_ref[...]).astype(BF16)
    hw = X_HEADS * X_HD
    outs = []
    for h in range(X_HEADS):
        kh = kv_ref[0, :, h * X_HD:(h + 1) * X_HD].astype(BF16)
        vh = kv_ref[0, :, hw + h * X_HD:hw + (h + 1) * X_HD].astype(BF16)
        s = _nt(q[:, h * X_HD:(h + 1) * X_HD], kh) * (X_HD ** -0.5)
        m = jnp.max(s, axis=-1, keepdims=True)
        p = jnp.exp(s - m)
        l = jnp.sum(p, axis=-1, keepdims=True)
        outs.append(_nn((p / l).astype(BF16), vh))
    o = jnp.concatenate(outs, axis=-1).astype(BF16)
    o_ref[0] = x + _nn(o, wo_ref[...])


def _cross(x3, g, wq, wo, kv3, *, tq=512):
    bsz, t, d = x3.shape
    tq = min(tq, t)
    hw = X_HEADS * X_HD
    return pl.pallas_call(
        _cross_body,
        grid=(bsz, t // tq),
        in_specs=[pl.BlockSpec((1, tq, d), lambda b, i: (b, i, 0)),
                  pl.BlockSpec((1, d), lambda b, i: (0, 0)),
                  pl.BlockSpec((d, hw), lambda b, i: (0, 0)),
                  pl.BlockSpec((hw, d), lambda b, i: (0, 0)),
                  pl.BlockSpec((1, MEM_LEN, 2 * hw), lambda b, i: (b, 0, 0))],
        out_specs=pl.BlockSpec((1, tq, d), lambda b, i: (b, i, 0)),
        out_shape=jax.ShapeDtypeStruct((bsz, t, d), F32),
        compiler_params=_cp("parallel", "parallel"),
        name="cross",
    )(x3, g.reshape(1, d), wq, wo, kv3)


def _cross_s_body(q_ref, kv_ref, o_ref):
    def one(j, carry):
        k = kv_ref[0, j, :, 0:X_HEADS, :]
        v = kv_ref[0, j, :, X_HEADS:2 * X_HEADS, :]
        q = q_ref[j]
        s = jnp.sum(k * q[None], axis=-1, keepdims=True) * (X_HD ** -0.5)
        m = jnp.max(s, axis=0, keepdims=True)
        p = jnp.exp(s - m)
        l = jnp.sum(p, axis=0, keepdims=True)
        o_ref[j] = jnp.sum((p / l) * v, axis=0)
        return carry

    lax.fori_loop(0, q_ref.shape[0], one, 0)


def _cross_s(q3, mem_kv5, layer, *, tb=8):
    nb = q3.shape[0]
    tb = min(tb, nb)
    return pl.pallas_call(
        _cross_s_body,
        grid=(nb // tb,),
        in_specs=[pl.BlockSpec((tb, X_HEADS, X_HD), lambda b: (b, 0, 0)),
                  pl.BlockSpec((1, tb, MEM_LEN, 2 * X_HEADS, X_HD), lambda b: (layer, b, 0, 0, 0))],
        out_specs=pl.BlockSpec((tb, X_HEADS, X_HD), lambda b: (b, 0, 0)),
        out_shape=jax.ShapeDtypeStruct((nb, X_HEADS, X_HD), F32),
        compiler_params=_cp("parallel"),
        name="cross_s",
    )(q3, mem_kv5)


def _rms_body(x_ref, g_ref, o_ref):
    o_ref[...] = _rms(x_ref[...], g_ref[...])


def _rmsnorm(x, g, *, tm=512):
    n, d = x.shape
    tm = min(tm, n)
    return pl.pallas_call(
        _rms_body,
        grid=(n // tm,),
        in_specs=[pl.BlockSpec((tm, d), lambda i: (i, 0)),
                  pl.BlockSpec((1, d), lambda i: (0, 0))],
        out_specs=pl.BlockSpec((tm, d), lambda i: (i, 0)),
        out_shape=jax.ShapeDtypeStruct((n, d), F32),
        compiler_params=_cp("parallel"),
        name="rmsnorm",
    )(x, g.reshape(1, d))


def _a_bias_vec(rel_bias, g):
    dist = A_DILATIONS[g] * jnp.arange(A_NKEYS, dtype=I32)
    return rel_bias[_rel_bucket(dist)][:, g * A_HEADS:(g + 1) * A_HEADS].astype(F32).T


def _a_prompt_body(q_ref, kc_ref, kp_ref, vc_ref, vp_ref, bias_ref, o_ref, lse_ref):
    j = pl.program_id(1)
    q = q_ref[0].astype(BF16)
    k = jnp.concatenate([kp_ref[0], kc_ref[0]], axis=0).astype(BF16)
    v = jnp.concatenate([vp_ref[0], vc_ref[0]], axis=0).astype(BF16)
    t = Q_TILE
    row = lax.broadcasted_iota(I32, (t, 2 * t), 0)
    col = lax.broadcasted_iota(I32, (t, 2 * t), 1)
    back = row + t - col
    ok = (back >= 0) & (back <= t) & ((col >= t) | (j > 0))
    outs, lses = [], []
    for h in range(A_HEADS):
        sl = slice(h * A_HD, (h + 1) * A_HD)
        s = _nt(q[:, sl], k[:, sl]) * (A_HD ** -0.5) + bias_ref[h]
        s = jnp.where(ok, s, NEG)
        m = jnp.max(s, axis=-1, keepdims=True)
        p = jnp.exp(s - m)
        l = jnp.sum(p, axis=-1, keepdims=True)
        outs.append(_nn(p.astype(BF16), v[:, sl]) / l)
        lses.append(jnp.broadcast_to(m + jnp.log(l), (t, A_HD)))
    o_ref[0] = jnp.concatenate(outs, axis=-1)
    lse_ref[0] = jnp.concatenate(lses, axis=-1)


def _a_prompt_group(qkv_g, bias_tile):
    nr, sj, _ = qkv_g.shape
    hw = A_HEADS * A_HD
    t = Q_TILE
    assert sj % t == 0

    def cur(which):
        return pl.BlockSpec((1, t, hw), lambda r, j: (r, j, which))

    def prev(which):
        return pl.BlockSpec((1, t, hw), lambda r, j: (r, jnp.maximum(j - 1, 0), which))

    out_spec = pl.BlockSpec((1, t, hw), lambda r, j: (r, j, 0))
    return pl.pallas_call(
        _a_prompt_body,
        grid=(nr, sj // t),
        in_specs=[cur(0), cur(1), prev(1), cur(2), prev(2),
                  pl.BlockSpec((A_HEADS, t, 2 * t), lambda r, j: (0, 0, 0))],
        out_specs=[out_spec, out_spec],
        out_shape=[jax.ShapeDtypeStruct((nr, sj, hw), F32)] * 2,
        compiler_params=_cp("parallel", "parallel"),
        name="a_prompt",
    )(qkv_g, qkv_g, qkv_g, qkv_g, qkv_g, bias_tile)


def _toeplitz(w, n_rows, n_cols):
    h, p = w.shape
    assert n_cols <= p - 1
    flat = jnp.tile(w, (1, n_rows))[:, :n_rows * (p - 1)]
    return flat.reshape(h, n_rows, p - 1)[:, :, :n_cols]


def _a_bias_tile(vec):
    t = Q_TILE
    nh = vec.shape[0]
    w = jnp.concatenate([vec[:, t:t + 1], jnp.zeros((nh, 2 * t - 1), F32), vec[:, :t]], axis=1)
    return jnp.transpose(_toeplitz(w, 2 * t, t), (0, 2, 1))


def _a_combine_body(o0, o1, o2, l0, l1, l2, x_ref, w_ref, out_ref):
    a0, a1, a2 = l0[...], l1[...], l2[...]
    m = jnp.maximum(jnp.maximum(a0, a1), a2)
    e0, e1, e2 = jnp.exp(a0 - m), jnp.exp(a1 - m), jnp.exp(a2 - m)
    den = e0 + e1 + e2
    o = (e0 / den) * o0[...] + (e1 / den) * o1[...] + (e2 / den) * o2[...]
    out_ref[...] = x_ref[...] + _nn(o.astype(BF16), w_ref[...])


def _a_combine(os_, ls_, x, w_out, *, tm=512):
    n, d = x.shape
    hw = w_out.shape[0]
    tm = min(tm, n)
    small = pl.BlockSpec((tm, hw), lambda i: (i, 0))
    return pl.pallas_call(
        _a_combine_body,
        grid=(n // tm,),
        in_specs=[small] * 6 + [pl.BlockSpec((tm, d), lambda i: (i, 0)),
                                pl.BlockSpec((hw, d), lambda i: (0, 0))],
        out_specs=pl.BlockSpec((tm, d), lambda i: (i, 0)),
        out_shape=jax.ShapeDtypeStruct((n, d), F32),
        compiler_params=_cp("parallel"),
        name="a_combine",
    )(*os_, *ls_, x, w_out)


def _row_to_col(r):
    n = r.shape[1]
    eye = (lax.broadcasted_iota(I32, (LANES, LANES), 0) == lax.broadcasted_iota(I32, (LANES, LANES), 1))
    cols = []
    for c in range(n // LANES):
        blk = jnp.broadcast_to(r[:, c * LANES:(c + 1) * LANES], (LANES, LANES))
        cols.append(jnp.sum(jnp.where(eye, blk, 0.0), axis=1, keepdims=True))
    return jnp.concatenate(cols, axis=0)


def _col_to_row(c):
    n = c.shape[0]
    eye = (lax.broadcasted_iota(I32, (LANES, LANES), 0) == lax.broadcasted_iota(I32, (LANES, LANES), 1))
    rows = []
    for i in range(n // LANES):
        blk = jnp.broadcast_to(c[i * LANES:(i + 1) * LANES, :], (LANES, LANES))
        rows.append(jnp.sum(jnp.where(eye, blk, 0.0), axis=0, keepdims=True))
    return jnp.concatenate(rows, axis=1)


def _a_sample_body(q_ref, kvn_ref, bias_ref, c_ref, o_ref, lse_ref, cn_ref, *, width):
    hw = A_HEADS * A_HD
    q_col = _row_to_col(q_ref[0])
    kvn_row = kvn_ref[0]
    kvn_col = _row_to_col(kvn_row)
    scale = A_HD ** -0.5
    s_rows, self_rows = [], []
    for h in range(A_HEADS):
        sl = slice(h * A_HD, (h + 1) * A_HD)
        kt = c_ref[0, sl, :]
        s_rows.append(jnp.sum(kt * q_col[sl], axis=0, keepdims=True))
        self_rows.append(jnp.sum(kvn_col[sl] * q_col[sl], axis=0, keepdims=True))
    bias = bias_ref[...]
    s = jnp.concatenate(s_rows, axis=0) * scale + bias[:, :width]
    s_self = jnp.concatenate(self_rows, axis=0) * scale + bias[:, width:width + 1]
    m = jnp.maximum(jnp.max(s, axis=-1, keepdims=True), s_self)
    p = jnp.exp(s - m)
    p_self = jnp.exp(s_self - m)
    l = jnp.sum(p, axis=-1, keepdims=True) + p_self
    lse = m + jnp.log(l)
    o_cols, lse_cols = [], []
    for h in range(A_HEADS):
        sl = slice(hw + h * A_HD, hw + (h + 1) * A_HD)
        vt = c_ref[0, sl, :]
        oc = jnp.sum(vt * p[h:h + 1, :], axis=1, keepdims=True) + kvn_col[sl] * p_self[h:h + 1, :]
        o_cols.append(oc / l[h:h + 1, :])
        lse_cols.append(jnp.broadcast_to(lse[h:h + 1, :], (A_HD, 1)))
    o_ref[0] = _col_to_row(jnp.concatenate(o_cols, axis=0))
    lse_ref[0] = _col_to_row(jnp.concatenate(lse_cols, axis=0))
    rc = 128
    lane = lax.broadcasted_iota(I32, (rc, width), 1)
    for r0 in range(0, 2 * hw, rc):
        rolled = pltpu.roll(c_ref[0, r0:r0 + rc, :], width - 1, 1)
        cn_ref[0, r0:r0 + rc, :] = jnp.where(lane == width - 1, kvn_col[r0:r0 + rc], rolled)


def _a_sample_group(q_g, kvn_g, cache_t, bias_s):
    bsz, rows, width = cache_t.shape
    hw = A_HEADS * A_HD
    body = functools.partial(_a_sample_body, width=width)
    return pl.pallas_call(
        body,
        grid=(bsz,),
        in_specs=[pl.BlockSpec((1, 1, hw), lambda b: (b, 0, 0)),
                  pl.BlockSpec((1, 1, 2 * hw), lambda b: (b, 0, 0)),
                  pl.BlockSpec((A_HEADS, width + LANES), lambda b: (0, 0)),
                  pl.BlockSpec((1, rows, width), lambda b: (b, 0, 0))],
        out_specs=[pl.BlockSpec((1, 1, hw), lambda b: (b, 0, 0)),
                   pl.BlockSpec((1, 1, hw), lambda b: (b, 0, 0)),
                   pl.BlockSpec((1, rows, width), lambda b: (b, 0, 0))],
        out_shape=[jax.ShapeDtypeStruct((bsz, 1, hw), F32),
                   jax.ShapeDtypeStruct((bsz, 1, hw), F32),
                   jax.ShapeDtypeStruct((bsz, rows, width), F32)],
        compiler_params=_cp("parallel"),
        name="a_sample",
    )(q_g, kvn_g, bias_s, cache_t)


def _mixer_a(xp, xs, bsz, seq, caches, rel_bias, g, a_w_in, a_w_out):
    hw = A_HEADS * A_HD
    w_in = a_w_in.astype(BF16)
    w_out = a_w_out.astype(BF16)
    qkv_s = _mm(xs, w_in, g=g, tn=w_in.shape[1] // 2)
    os_p, ls_p, os_s, ls_s, new_p, new_s = [], [], [], [], [], []
    nb = xs.shape[0]
    dm = xp.shape[1]
    for gi in range(3):
        vec = _a_bias_vec(rel_bias, gi)
        d = A_DILATIONS[gi]
        sj = seq // d
        x_g = jnp.transpose(xp.reshape(bsz, sj, d, dm), (0, 2, 1, 3)).reshape(bsz * seq, dm) if d > 1 else xp
        w_g = jnp.concatenate([w_in[:, (3 * j + gi) * hw:(3 * j + gi + 1) * hw] for j in range(3)], axis=1)
        qkv_g = _mm(x_g, w_g, g=g).reshape(bsz * d, sj, 3 * hw)
        o, lse = _a_prompt_group(qkv_g, _a_bias_tile(vec))

        def natural(a):
            return jnp.transpose(a.reshape(bsz, d, sj, -1), (0, 2, 1, 3)).reshape(bsz * seq, -1)

        os_p.append(natural(o))
        ls_p.append(natural(lse))
        w = min(A_WINDOWS[gi], seq)
        assert w % d == 0
        kv_tail = qkv_g.reshape(bsz, d, sj, 3 * hw)[:, :, sj - w // d:, hw:]
        new_p.append(jnp.transpose(kv_tail, (0, 2, 1, 3)).reshape(bsz, w, 2, A_HEADS, A_HD))
        cache = caches[gi]
        width = cache.shape[1]
        assert width == (A_NKEYS - 1) * d
        cache_t = jnp.transpose(cache, (0, 2, 3, 4, 1)).reshape(nb, 2 * hw, width)
        q_s = qkv_s[:, gi * hw:(gi + 1) * hw].reshape(nb, 1, hw)
        kvn = jnp.concatenate([qkv_s[:, (3 + gi) * hw:(4 + gi) * hw],
                               qkv_s[:, (6 + gi) * hw:(7 + gi) * hw]], axis=1).reshape(nb, 1, 2 * hw)
        lane = jnp.arange(width)
        kk = (width - lane) // d
        bias_c = jnp.where((lane % d == 0)[None, :], vec[:, jnp.clip(kk, 0, A_NKEYS - 1)], NEG)
        bias_s = jnp.concatenate([bias_c, jnp.broadcast_to(vec[:, 0:1], (A_HEADS, LANES))], axis=1)
        o_s, lse_s, cache_new = _a_sample_group(q_s, kvn, cache_t, bias_s)
        os_s.append(o_s.reshape(nb, hw))
        ls_s.append(lse_s.reshape(nb, hw))
        new_s.append(jnp.transpose(cache_new.reshape(nb, 2, A_HEADS, A_HD, width), (0, 4, 1, 2, 3)))
    xp = _a_combine(os_p, ls_p, xp, w_out)
    xs = _a_combine(os_s, ls_s, xs, w_out)
    return xp, xs, new_p, new_s


def _ln_silu_proj(c, lng_ref, lnb_ref, w2_ref, b2_ref, x):
    mu = jnp.mean(c, axis=-1, keepdims=True)
    xc = c - mu
    y = xc * lax.rsqrt(jnp.mean(xc * xc, axis=-1, keepdims=True) + EPS)
    y = y * lng_ref[...] + lnb_ref[...]
    y = y * _sigmoid(y)
    return x + _nn(y.astype(BF16), w2_ref[...]) + b2_ref[...]


def _b_prompt_body(uc_ref, up_ref, x_ref, wdw_ref, bdw_ref, lng_ref, lnb_ref, w2_ref, b2_ref,
                   o_ref, buf_ref, c_ref, *, tq):
    i = pl.program_id(1)
    pad = 32
    buf_ref[0:pad, :] = jnp.where(i > 0, up_ref[0], 0.0)
    buf_ref[pad:pad + tq, :] = uc_ref[0]
    off = pad - (B_CONV_WIDTH - 1)
    rc, cc = 128, 256
    d = uc_ref.shape[2]
    for r0 in range(0, tq, rc):
        for c0 in range(0, d, cc):
            acc = jnp.zeros((rc, cc), F32)
            for w in range(B_CONV_WIDTH):
                acc = acc + buf_ref[r0 + off + w:r0 + off + w + rc, c0:c0 + cc] * wdw_ref[w:w + 1, c0:c0 + cc]
            c_ref[r0:r0 + rc, c0:c0 + cc] = acc
    c = c_ref[...] + bdw_ref[...]
    o_ref[0] = _ln_silu_proj(c, lng_ref, lnb_ref, w2_ref, b2_ref, x_ref[0])


def _b_prompt(u3, x3, wdw, bdw, lng, lnb, w2, b2, *, tq=256):
    bsz, s, d = u3.shape
    tq = min(tq, s)
    pad = 32
    body = functools.partial(_b_prompt_body, tq=tq)
    vec = lambda: pl.BlockSpec((1, d), lambda b, i: (0, 0))
    return pl.pallas_call(
        body,
        grid=(bsz, s // tq),
        in_specs=[pl.BlockSpec((1, tq, d), lambda b, i: (b, i, 0)),
                  pl.BlockSpec((1, pad, d), lambda b, i: (b, jnp.maximum(i * (tq // pad) - 1, 0), 0)),
                  pl.BlockSpec((1, tq, d), lambda b, i: (b, i, 0)),
                  pl.BlockSpec((B_CONV_WIDTH, d), lambda b, i: (0, 0)),
                  vec(), vec(), vec(),
                  pl.BlockSpec((d, d), lambda b, i: (0, 0)),
                  vec()],
        out_specs=pl.BlockSpec((1, tq, d), lambda b, i: (b, i, 0)),
        out_shape=jax.ShapeDtypeStruct((bsz, s, d), F32),
        scratch_shapes=[pltpu.VMEM((tq + pad, d), F32), pltpu.VMEM((tq, d), F32)],
        compiler_params=_cp("parallel", "parallel"),
        name="b_prompt",
    )(u3, u3, x3, wdw, bdw.reshape(1, d), lng.reshape(1, d), lnb.reshape(1, d), w2, b2.reshape(1, d))


def _b_sample_body(st_ref, u_ref, x_ref, wdw_ref, bdw_ref, lng_ref, lnb_ref, w2_ref, b2_ref,
                   o_ref, ns_ref):
    nw = B_CONV_WIDTH - 1
    u = u_ref[...]
    acc = u * wdw_ref[nw:nw + 1, :] + bdw_ref[...]
    for w in range(nw):
        acc = acc + st_ref[w] * wdw_ref[w:w + 1, :]
        if w > 0:
            ns_ref[w - 1] = st_ref[w]
    ns_ref[nw - 1] = u
    o_ref[...] = _ln_silu_proj(acc, lng_ref, lnb_ref, w2_ref, b2_ref, x_ref[...])


def _b_sample(state_t, u, x, wdw, bdw, lng, lnb, w2, b2, *, tb=32):
    nw, nb, d = state_t.shape
    tb = min(tb, nb)
    vec = lambda: pl.BlockSpec((1, d), lambda i: (0, 0))
    return pl.pallas_call(
        _b_sample_body,
        grid=(nb // tb,),
        in_specs=[pl.BlockSpec((nw, tb, d), lambda i: (0, i, 0)),
                  pl.BlockSpec((tb, d), lambda i: (i, 0)),
                  pl.BlockSpec((tb, d), lambda i: (i, 0)),
                  pl.BlockSpec((B_CONV_WIDTH, d), lambda i: (0, 0)),
                  vec(), vec(), vec(),
                  pl.BlockSpec((d, d), lambda i: (0, 0)),
                  vec()],
        out_specs=[pl.BlockSpec((tb, d), lambda i: (i, 0)),
                   pl.BlockSpec((nw, tb, d), lambda i: (0, i, 0))],
        out_shape=[jax.ShapeDtypeStruct((nb, d), F32),
                   jax.ShapeDtypeStruct((nw, nb, d), F32)],
        compiler_params=_cp("parallel"),
        name="b_sample",
    )(state_t, u, x, wdw, bdw.reshape(1, d), lng.reshape(1, d), lnb.reshape(1, d), w2, b2.reshape(1, d))


def _mixer_b(xp, xs, bsz, seq, state, g, w_pw1, b_pw1, w_dw, b_dw, ln_g, ln_b, w_pw2, b_pw2):
    d = xp.shape[1]
    wa = w_pw1[:, :d].astype(BF16)
    wb = w_pw1[:, d:].astype(BF16)
    w2 = w_pw2.astype(BF16)
    nw = B_CONV_WIDTH - 1
    u_p = _mm(xp, wa, g=g, b=b_pw1[:d], w2=wb, b2=b_pw1[d:])
    u_s = _mm(xs, wa, g=g, b=b_pw1[:d], w2=wb, b2=b_pw1[d:])
    u3 = u_p.reshape(bsz, seq, d)
    xp = _b_prompt(u3, xp.reshape(bsz, seq, d), w_dw, b_dw, ln_g, ln_b, w2, b_pw2).reshape(bsz * seq, d)
    conv_p = u3[:, seq - nw:]
    xs, ns_t = _b_sample(jnp.transpose(state, (1, 0, 2)), u_s, xs, w_dw, b_dw, ln_g, ln_b, w2, b_pw2)
    return xp, xs, conv_p, jnp.transpose(ns_t, (1, 0, 2))


C_QW = 128
C_DOWN = C_Q_LORA + C_KV_LORA


def _c_prep_body(h_ref, ct_ref, st_ref, gq_ref, gkv_ref, wqa_ref, wqb_ref, wuk_ref, wuv_ref,
                 q_ref, k_ref, v_ref, lat_ref, kr_ref):
    h = h_ref[...]
    ct = ct_ref[...]
    st = st_ref[...]
    cq = _rms(h[:, :C_Q_LORA], gq_ref[...]).astype(BF16)
    lat = _rms(h[:, C_Q_LORA:C_DOWN], gkv_ref[...])
    lat_ref[...] = lat
    lat_b = lat.astype(BF16)
    krp = h[:, C_DOWN:C_DOWN + C_QW] * ct + h[:, C_DOWN + C_QW:C_DOWN + 2 * C_QW] * st
    kr_ref[...] = krp[:, C_NOPE:C_NOPE + C_ROPE]
    qa = _nn(cq, wqa_ref[...])
    qb = _nn(cq, wqb_ref[...])
    kn = _nn(lat_b, wuk_ref[...])
    for hd in range(C_HEADS):
        sl = slice(hd * C_QW, (hd + 1) * C_QW)
        q_ref[:, sl] = (qa[:, sl] * ct + qb[:, sl] * st).astype(BF16)
        k_ref[:, sl] = (kn[:, sl] + krp).astype(BF16)
    v_ref[...] = _nn(lat_b, wuv_ref[...]).astype(BF16)


def _c_prep(h, ct, st, gq, gkv, wqa, wqb, wuk, wuv, *, tm=512):
    n, hc = h.shape
    tm = min(tm, n)
    nt = ct.shape[0] // tm
    qw = C_HEADS * C_QW
    vw = C_HEADS * C_V
    full = lambda a: pl.BlockSpec(a.shape, lambda i: (0, 0))
    return pl.pallas_call(
        _c_prep_body,
        grid=(n // tm,),
        in_specs=[pl.BlockSpec((tm, hc), lambda i: (i, 0)),
                  pl.BlockSpec((tm, C_QW), lambda i: (i % nt, 0)),
                  pl.BlockSpec((tm, C_QW), lambda i: (i % nt, 0)),
                  full(gq), full(gkv), full(wqa), full(wqb), full(wuk), full(wuv)],
        out_specs=[pl.BlockSpec((tm, qw), lambda i: (i, 0)),
                   pl.BlockSpec((tm, qw), lambda i: (i, 0)),
                   pl.BlockSpec((tm, vw), lambda i: (i, 0)),
                   pl.BlockSpec((tm, C_KV_LORA), lambda i: (i, 0)),
                   pl.BlockSpec((tm, C_ROPE), lambda i: (i, 0))],
        out_shape=[jax.ShapeDtypeStruct((n, qw), BF16),
                   jax.ShapeDtypeStruct((n, qw), BF16),
                   jax.ShapeDtypeStruct((n, vw), BF16),
                   jax.ShapeDtypeStruct((n, C_KV_LORA), F32),
                   jax.ShapeDtypeStruct((n, C_ROPE), F32)],
        compiler_params=_cp("parallel"),
        name="c_prep",
    )(h, ct, st, gq, gkv, wqa, wqb, wuk, wuv)


def _c_flash_body(q_ref, k_ref, vt_ref, ot_ref, m_ref, l_ref, acc_ref, s_ref, p_ref, a_ref, *, tq, scale):
    i = pl.program_id(2)
    t = C_KC
    m_ref[...] = jnp.full(m_ref.shape, NEG, F32)
    l_ref[...] = jnp.zeros(l_ref.shape, F32)
    acc_ref[...] = jnp.zeros(acc_ref.shape, F32)
    q = q_ref[0]
    krow = lax.broadcasted_iota(I32, (t, tq), 0)
    qcol = lax.broadcasted_iota(I32, (t, tq), 1)
    n_full = i * (tq // t)
    scale2 = scale * math.log2(math.e)

    def chunk(c, masked):
        off = pl.multiple_of(c * t, t)
        kc = k_ref[0, pl.ds(off, t), :]
        vtc = vt_ref[0, :, pl.ds(off, t)]
        for hh in range(C_GRP):
            s_ref[hh] = _nt(kc[:, hh * C_QW:(hh + 1) * C_QW], q[:, hh * C_QW:(hh + 1) * C_QW])
        for hh in range(C_GRP):
            s = s_ref[hh] * scale2
            if masked:
                s = jnp.where(off + krow <= i * tq + qcol, s, NEG)
            s_ref[hh] = s
            m_old = m_ref[hh]
            m_new = jnp.maximum(m_old, jnp.max(s, axis=0, keepdims=True))
            a_ref[hh] = jnp.exp2(m_old - m_new)
            m_ref[hh] = m_new
        for hh in range(C_GRP):
            p = jnp.exp2(s_ref[hh] - m_ref[hh])
            l_ref[hh] = a_ref[hh] * l_ref[hh] + jnp.sum(p, axis=0, keepdims=True)
            p_ref[hh] = p.astype(BF16)
        for hh in range(C_GRP):
            rows = slice(hh * C_V, (hh + 1) * C_V)
            acc_ref[rows, :] = a_ref[hh] * acc_ref[rows, :] + _nn(vtc[rows, :], p_ref[hh])

    def full_chunk(c, carry):
        chunk(c, False)
        return carry

    lax.fori_loop(0, n_full, full_chunk, 0)
    for dc in range(tq // t):
        chunk(n_full + dc, True)
    for hh in range(C_GRP):
        rows = slice(hh * C_V, (hh + 1) * C_V)
        ot_ref[0, rows, :] = (acc_ref[rows, :] / l_ref[hh]).astype(ot_ref.dtype)


C_GRP = 4
C_KC = 256


def _c_flash(q3, k3, vt3, *, tq=512):
    bsz, s, _ = q3.shape
    tq = min(tq, s)
    assert tq % C_KC == 0
    ngrp = C_HEADS // C_GRP
    scale = (C_NOPE + C_ROPE) ** -0.5
    body = functools.partial(_c_flash_body, tq=tq, scale=scale)
    return pl.pallas_call(
        body,
        grid=(bsz, ngrp, s // tq),
        in_specs=[pl.BlockSpec((1, tq, C_GRP * C_QW), lambda b, hp, i: (b, i, hp)),
                  pl.BlockSpec((1, s, C_GRP * C_QW), lambda b, hp, i: (b, 0, hp)),
                  pl.BlockSpec((1, C_GRP * C_V, s), lambda b, hp, i: (b, hp, 0))],
        out_specs=pl.BlockSpec((1, C_GRP * C_V, tq), lambda b, hp, i: (b, hp, i)),
        out_shape=jax.ShapeDtypeStruct((bsz, C_HEADS * C_V, s), BF16),
        scratch_shapes=[pltpu.VMEM((C_GRP, 1, tq), F32), pltpu.VMEM((C_GRP, 1, tq), F32),
                        pltpu.VMEM((C_GRP * C_V, tq), F32),
                        pltpu.VMEM((C_GRP, C_KC, tq), F32), pltpu.VMEM((C_GRP, C_KC, tq), BF16),
                        pltpu.VMEM((C_GRP, 1, tq), F32)],
        compiler_params=_cp("parallel", "parallel", "arbitrary"),
        name="c_flash",
    )(q3, k3, vt3)


def _head_mm_body(x_ref, w_ref, o_ref):
    o_ref[...] = _nn(x_ref[...], w_ref[0]).astype(o_ref.dtype)


def _head_mm(x, w3, *, out_dtype=F32):
    n = x.shape[0]
    nh, kin, kout = w3.shape
    return pl.pallas_call(
        _head_mm_body,
        grid=(nh,),
        in_specs=[pl.BlockSpec((n, kin), lambda h: (0, h)),
                  pl.BlockSpec((1, kin, kout), lambda h: (h, 0, 0))],
        out_specs=pl.BlockSpec((n, kout), lambda h: (0, h)),
        out_shape=jax.ShapeDtypeStruct((n, nh * kout), out_dtype),
        compiler_params=_cp("parallel"),
        name="head_mm",
    )(x, w3)


PG = 16


def _paged_fetch(pt_ref, n_pages, pairs, sem):
    b = pl.program_id(0)
    slot = lax.rem(b, 2)

    def page_copy(bb, sl, p, k):
        src, buf = pairs[k]
        return pltpu.make_async_copy(src.at[pt_ref[bb * n_pages + p]], buf.at[sl, p], sem.at[k, sl])

    def issue(bb, sl):
        def body(p, carry):
            for k in range(len(pairs)):
                page_copy(bb, sl, p, k).start()
            return carry
        lax.fori_loop(0, n_pages, body, 0)

    @pl.when(b == 0)
    def _():
        issue(b, slot)

    @pl.when(b + 1 < pl.num_programs(0))
    def _():
        issue(b + 1, 1 - slot)

    def wait_body(p, carry):
        for k in range(len(pairs)):
            page_copy(b, slot, p, k).wait()
        return carry

    lax.fori_loop(0, n_pages, wait_body, 0)
    return slot


def _c_decode_body(pt_ref, lat_hbm, rope_hbm, ql_ref, qr_ref, latn_ref, krn_ref, o_ref,
                   lat_buf, rope_buf, sem, *, n_pages, scale):
    slot = _paged_fetch(pt_ref, n_pages, [(lat_hbm, lat_buf), (rope_hbm, rope_buf)], sem)
    ql = ql_ref[0]
    qr = qr_ref[0]

    def group(gi, carry):
        m_old, l_old, acc = carry
        p0 = gi * PG
        lat = lat_buf[slot, pl.ds(p0, PG)].reshape(PG * PAGE, C_KV_LORA).astype(BF16)
        s_rope = jnp.concatenate([_nn(qr, rope_buf[slot, p0 + g].astype(BF16)) for g in range(PG)], axis=1)
        s = (_nt(ql, lat) + s_rope) * scale
        m_new = jnp.maximum(m_old, jnp.max(s, axis=-1, keepdims=True))
        alpha = jnp.exp(m_old - m_new)
        p = jnp.exp(s - m_new)
        return (m_new, alpha * l_old + jnp.sum(p, axis=-1, keepdims=True),
                alpha * acc + _nn(p.astype(BF16), lat))

    init = (jnp.full((C_HEADS, 1), NEG, F32), jnp.zeros((C_HEADS, 1), F32),
            jnp.zeros((C_HEADS, C_KV_LORA), F32))
    m_o, l_o, acc_o = lax.fori_loop(0, n_pages // PG, group, init)
    latn = latn_ref[0].astype(BF16)
    krn = krn_ref[0].astype(BF16)
    s_self = (jnp.sum(ql.astype(F32) * latn.astype(F32), axis=1, keepdims=True)
              + jnp.sum(qr.astype(F32) * krn.astype(F32), axis=1, keepdims=True)) * scale
    m_n = jnp.maximum(m_o, s_self)
    a = jnp.exp(m_o - m_n)
    p_s = jnp.exp(s_self - m_n)
    l = a * l_o + p_s
    acc = a * acc_o + p_s.astype(BF16).astype(F32) * latn.astype(F32)
    o_ref[0] = (acc / l).astype(o_ref.dtype)


def _c_decode(page_flat, n_pages, lat_cache, rope_cache_t, ql3, qr3, latn3, krn3):
    nb = ql3.shape[0]
    scale = (C_NOPE + C_ROPE) ** -0.5
    assert n_pages % PG == 0
    per = lambda shp: pl.BlockSpec((1,) + shp, lambda b, pt: (b, 0, 0))
    hbm = pl.BlockSpec(memory_space=pl.ANY)
    gs = pltpu.PrefetchScalarGridSpec(
        num_scalar_prefetch=1,
        grid=(nb,),
        in_specs=[hbm, hbm, per((C_HEADS, C_KV_LORA)), per((C_HEADS, C_ROPE)),
                  per((1, C_KV_LORA)), per((1, C_ROPE))],
        out_specs=per((C_HEADS, C_KV_LORA)),
        scratch_shapes=[pltpu.VMEM((2, n_pages, PAGE, C_KV_LORA), F32),
                        pltpu.VMEM((2, n_pages, C_ROPE, PAGE), F32),
                        pltpu.SemaphoreType.DMA((2, 2))],
    )
    return pl.pallas_call(
        functools.partial(_c_decode_body, n_pages=n_pages, scale=scale),
        grid_spec=gs,
        out_shape=jax.ShapeDtypeStruct((nb, C_HEADS, C_KV_LORA), BF16),
        compiler_params=_cp("arbitrary"),
        name="c_decode",
    )(page_flat, lat_cache, rope_cache_t, ql3, qr3, latn3, krn3)


def _rope_tables(pos):
    inv = ROPE_THETA ** (-jnp.arange(0, C_ROPE, 2, dtype=F32) / C_ROPE)
    ang = pos.astype(F32)[:, None] * inv[None, :]
    cos, sin = jnp.cos(ang), jnp.sin(ang)
    n = pos.shape[0]
    ones = jnp.ones((n, C_NOPE), F32)
    zeros = jnp.zeros((n, C_NOPE), F32)
    tail = jnp.zeros((n, C_QW - C_NOPE - C_ROPE), F32)
    ct = jnp.concatenate([ones, cos, cos, tail], axis=1)
    st = jnp.concatenate([zeros, -sin, sin, tail], axis=1)
    return ct, st


def _rope_swap(w):
    half = C_ROPE // 2
    return jnp.concatenate([w[..., half:], w[..., :half]], axis=-1)


def _pad_cols(w, left, total):
    return jnp.pad(w, ((0, 0), (left, total - left - w.shape[1])))


def _mixer_c(xp, xs, bsz, seq, lat_cache, rope_cache, page_table, g,
             c_w_down, c_g_q, c_g_kv, c_w_uq, c_w_uk, c_w_uv, c_w_out):
    nb = xs.shape[0]
    n_pages = page_table.shape[1]
    past = n_pages * PAGE
    w_rope = c_w_down[:, C_DOWN:]
    w_down = jnp.concatenate([c_w_down[:, :C_DOWN], _pad_cols(w_rope, C_NOPE, C_QW),
                              _pad_cols(_rope_swap(w_rope), C_NOPE, C_QW)], axis=1).astype(BF16)
    uq = c_w_uq.reshape(C_Q_LORA, C_HEADS, C_NOPE + C_ROPE)
    zq = jnp.zeros((C_Q_LORA, C_HEADS, C_QW - C_NOPE - C_ROPE), F32)
    wqa = jnp.concatenate([uq, zq], axis=2).reshape(C_Q_LORA, C_HEADS * C_QW).astype(BF16)
    wqb = jnp.concatenate([jnp.zeros((C_Q_LORA, C_HEADS, C_NOPE), F32), _rope_swap(uq[..., C_NOPE:]), zq],
                          axis=2).reshape(C_Q_LORA, C_HEADS * C_QW).astype(BF16)
    wuk = jnp.concatenate([c_w_uk, jnp.zeros((C_KV_LORA, C_HEADS, C_QW - C_NOPE), F32)],
                          axis=2).reshape(C_KV_LORA, C_HEADS * C_QW).astype(BF16)
    wuv = c_w_uv.reshape(C_KV_LORA, C_HEADS * C_V).astype(BF16)
    w_out = c_w_out.astype(BF16)
    gq = c_g_q.reshape(1, -1)
    gkv = c_g_kv.reshape(1, -1)

    h_p = _mm(xp, w_down, g=g)
    ct_p, st_p = _rope_tables(jnp.arange(seq, dtype=I32))
    q, k, v, lat_p, kr_p = _c_prep(h_p, ct_p, st_p, gq, gkv, wqa, wqb, wuk, wuv)
    vt = jnp.transpose(v.reshape(bsz, seq, -1), (0, 2, 1))
    ot = _c_flash(q.reshape(bsz, seq, -1), k.reshape(bsz, seq, -1), vt)
    xp = _mm(jnp.transpose(ot, (0, 2, 1)).reshape(bsz * seq, -1), w_out, res=xp)

    h_s = _mm(xs, w_down, g=g)
    ct_s, st_s = _rope_tables(jnp.full((nb,), past, I32))
    q_s, _, _, lat_s, kr_s = _c_prep(h_s, ct_s, st_s, gq, gkv, wqa, wqb, wuk, wuv)
    wukt = jnp.concatenate([jnp.transpose(c_w_uk, (1, 2, 0)),
                            jnp.zeros((C_HEADS, C_QW - C_NOPE, C_KV_LORA), F32)], axis=1).astype(BF16)
    ql = _head_mm(q_s, wukt, out_dtype=BF16).reshape(nb, C_HEADS, C_KV_LORA)
    qr = q_s.reshape(nb, C_HEADS, C_QW)[:, :, C_NOPE:C_NOPE + C_ROPE]
    o_lat = _c_decode(page_table.reshape(-1), n_pages, lat_cache, jnp.transpose(rope_cache, (0, 2, 1)),
                      ql, qr, lat_s.reshape(nb, 1, -1), kr_s.reshape(nb, 1, -1))
    wuv2 = jnp.transpose(c_w_uv, (1, 0, 2)).reshape(C_HEADS // 2, 2, C_KV_LORA, C_V)
    z = jnp.zeros_like(wuv2[:, 0])
    wuv_pair = jnp.concatenate([jnp.concatenate([wuv2[:, 0], z], axis=2),
                                jnp.concatenate([z, wuv2[:, 1]], axis=2)], axis=1).astype(BF16)
    o_s = _head_mm(o_lat.reshape(nb, C_HEADS * C_KV_LORA), wuv_pair, out_dtype=BF16)
    xs = _mm(o_s, w_out, res=xs)
    return (xp, xs, lat_p.reshape(bsz, seq, -1), kr_p.reshape(bsz, seq, -1),
            lat_s.reshape(nb, 1, -1), kr_s.reshape(nb, 1, -1))


D_Q0 = 0
D_KV0 = D_HEADS * D_HD
D_QI0 = D_KV0 + 2 * D_KV_HEADS * D_HD
D_KI0 = D_QI0 + D_IDX_HEADS * D_IDX_DIM
D_WI0 = D_KI0 + D_IDX_DIM
D_HW = D_WI0 + LANES - D_IDX_DIM
INT_MIN = -2 ** 31


def _sort_key(score):
    bits = pltpu.bitcast(score, I32)
    return jnp.where(bits >= 0, bits, bits ^ jnp.int32(0x7FFFFFFF))


CNT_UNROLL = 4
D_TQ = 256


def _kth_largest(key_ref, n_groups, k):
    nq = key_ref.shape[1]
    kf = jnp.float32(k)

    def count(pred_fn):
        def body(gi, acc):
            for u in range(CNT_UNROLL):
                off = pl.multiple_of((gi * CNT_UNROLL + u) * LANES, LANES)
                hit = jnp.where(pred_fn(key_ref[pl.ds(off, LANES), :], off), 1.0, 0.0)
                acc = acc + jnp.sum(hit.reshape(LANES // 8, 8, nq), axis=0)
            return acc
        part = lax.fori_loop(0, n_groups, body, jnp.zeros((8, nq), F32))
        return jnp.sum(part, axis=0, keepdims=True)

    nonneg = count(lambda x, off: x >= 0) >= kf
    prefix = jnp.where(nonneg, jnp.int32(0), jnp.int32(INT_MIN))

    def bit_body(i, prefix):
        cand = prefix | lax.shift_left(jnp.int32(1), 30 - i)
        return jnp.where(count(lambda x, off: x >= cand) >= kf, cand, prefix)

    thr = lax.fori_loop(0, 31, bit_body, prefix)
    n_gt = count(lambda x, off: x > thr)
    n_eq = count(lambda x, off: x == thr)
    need = kf - n_gt

    def tie_search(_):
        row = lax.broadcasted_iota(I32, (LANES, nq), 0)

        def body(i, lo):
            cand = lo + lax.shift_right_logical(jnp.int32(1 << 29), i)
            c = count(lambda x, off: (x == thr) & (row + off < cand))
            return jnp.where(c < need, cand, lo)
        return lax.fori_loop(0, 30, body, jnp.zeros((1, nq), I32)) + 1

    any_tie = jnp.max(jnp.where(n_eq > need, 1.0, 0.0)) > 0.0
    bound = lax.cond(any_tie, tie_search, lambda _: jnp.full((1, nq), 2 ** 30, I32), 0)
    return thr, bound


def _d_prompt_body(q_ref, qi_ref, wit_ref, k_ref, vt_ref, ki_ref, bank_ref, ot_ref,
                   key_ref, m_ref, l_ref, acc_ref, s_ref, p_ref, a_ref, *, k_top, n_bank):
    t = Q_TILE
    tq = D_TQ
    nsub = tq // t
    blk = pl.program_id(1)
    n_chunks = (blk + 1) * nsub
    krow = lax.broadcasted_iota(I32, (t, tq), 0)
    qpos = blk * tq + lax.broadcasted_iota(I32, (t, tq), 1)
    qi = qi_ref[0]
    wit = wit_ref[0] * (D_IDX_HEADS ** -0.5) * (D_IDX_DIM ** -0.5)

    def score_chunk(c, carry):
        off = pl.multiple_of(c * t, t)
        ki = ki_ref[0, pl.ds(off, t), :][:, :D_IDX_DIM].astype(BF16)
        dots = _nt(ki, qi)
        sc = jnp.zeros((t, tq), F32)
        for h in range(D_IDX_HEADS):
            sc = sc + wit[h:h + 1, :] * jnp.maximum(dots[:, h * tq:(h + 1) * tq], 0.0)
        key = _sort_key(sc)
        key_ref[pl.ds(off, t), :] = jnp.where(off + krow <= qpos, key, jnp.int32(INT_MIN))
        return carry

    lax.fori_loop(0, n_chunks, score_chunk, 0)
    n_groups = (n_chunks + CNT_UNROLL - 1) // CNT_UNROLL

    def pad_chunk(c, carry):
        key_ref[pl.ds(pl.multiple_of(c * t, t), t), :] = jnp.full((t, tq), INT_MIN, I32)
        return carry

    lax.fori_loop(n_chunks, n_groups * CNT_UNROLL, pad_chunk, 0)
    thr, bound = _kth_largest(key_ref, n_groups, k_top)
    thr = jnp.maximum(thr, jnp.int32(INT_MIN + 1))

    m_ref[...] = jnp.full(m_ref.shape, M_INIT, F32)
    l_ref[...] = jnp.zeros(l_ref.shape, F32)
    acc_ref[...] = jnp.zeros(acc_ref.shape, F32)
    q = (q_ref[0] * (D_HD ** -0.5)).astype(BF16)
    rep = D_HEADS // D_KV_HEADS

    def attend_chunk(c, carry):
        off = pl.multiple_of(c * t, t)
        key = key_ref[pl.ds(off, t), :]
        sel = (key > thr) | ((key == thr) & (krow + off < bound))
        kc = k_ref[0, pl.ds(off, t), :]
        vtc = vt_ref[0, :, pl.ds(off, t)]
        ds_ = [jnp.clip(blk * nsub + u - c, 0, n_bank - 1) for u in range(nsub)]
        for h in range(D_HEADS):
            gk = h // rep
            s_ref[h] = _nt(kc[:, gk * D_HD:(gk + 1) * D_HD], q[:, h * D_HD:(h + 1) * D_HD])
        for h in range(D_HEADS):
            bias = jnp.concatenate([bank_ref[du, h] for du in ds_], axis=1)
            s = jnp.where(sel, s_ref[h] + bias, NEG)
            s_ref[h] = s
            m_old = m_ref[h]
            m_new = jnp.maximum(m_old, jnp.max(s, axis=0, keepdims=True))
            a_ref[h] = jnp.exp(m_old - m_new)
            m_ref[h] = m_new
        for h in range(D_HEADS):
            p = jnp.exp(s_ref[h] - m_ref[h])
            l_ref[h] = a_ref[h] * l_ref[h] + jnp.sum(p, axis=0, keepdims=True)
            p_ref[h] = p.astype(BF16)
        for h in range(D_HEADS):
            gk = h // rep
            acc_ref[h] = a_ref[h] * acc_ref[h] + _nn(vtc[gk * D_HD:(gk + 1) * D_HD, :], p_ref[h])
        return carry

    lax.fori_loop(0, n_chunks, attend_chunk, 0)
    for h in range(D_HEADS):
        ot_ref[0, h * D_HD:(h + 1) * D_HD, :] = (acc_ref[h] / l_ref[h]).astype(ot_ref.dtype)


def _d_prompt(h3, qir, wit, kb, vt, bank, k_top):
    bsz, seq, _ = h3.shape
    t = Q_TILE
    tq = D_TQ
    assert seq % (CNT_UNROLL * t) == 0 and seq % tq == 0
    n_bank = bank.shape[0]
    body = functools.partial(_d_prompt_body, k_top=k_top, n_bank=n_bank)
    qw = D_HEADS * D_HD
    kw = D_KV_HEADS * D_HD
    return pl.pallas_call(
        body,
        grid=(bsz, seq // tq),
        in_specs=[pl.BlockSpec((1, tq, qw), lambda b, i: (b, i, 0)),
                  pl.BlockSpec((1, D_IDX_HEADS * tq, D_IDX_DIM), lambda b, i: (b, i, 0)),
                  pl.BlockSpec((1, D_IDX_HEADS, tq), lambda b, i: (b, 0, i)),
                  pl.BlockSpec((1, seq, kw), lambda b, i: (b, 0, 0)),
                  pl.BlockSpec((1, kw, seq), lambda b, i: (b, 0, 0)),
                  pl.BlockSpec((1, seq, LANES), lambda b, i: (b, 0, D_KI0 // LANES)),
                  pl.BlockSpec(bank.shape, lambda b, i: (0, 0, 0, 0), pipeline_mode=pl.Buffered(1))],
        out_specs=pl.BlockSpec((1, qw, tq), lambda b, i: (b, 0, i)),
        out_shape=jax.ShapeDtypeStruct((bsz, qw, seq), BF16),
        scratch_shapes=[pltpu.VMEM((seq, tq), I32),
                        pltpu.VMEM((D_HEADS, 1, tq), F32), pltpu.VMEM((D_HEADS, 1, tq), F32),
                        pltpu.VMEM((D_HEADS, D_HD, tq), F32),
                        pltpu.VMEM((D_HEADS, t, tq), F32), pltpu.VMEM((D_HEADS, t, tq), BF16),
                        pltpu.VMEM((D_HEADS, 1, tq), F32)],
        compiler_params=_cp("parallel", "arbitrary"),
        name="d_prompt",
    )(h3, qir, wit, kb, vt, h3, bank)


def _d_index_body(pt_ref, kidx_hbm, qi_ref, wi_ref, kin_ref, sc_ref, self_ref, page_buf, sem, *, n_pages):
    slot = _paged_fetch(pt_ref, n_pages, [(kidx_hbm, page_buf)], sem)
    qi = qi_ref[0]
    wi = wi_ref[0] * (D_IDX_HEADS ** -0.5)
    qb = qi.astype(BF16)
    width = PG * PAGE

    def group(gi, carry):
        parts = []
        for g in range(PG):
            dots = _nn(qb, page_buf[slot, gi * PG + g].astype(BF16)) * (D_IDX_DIM ** -0.5)
            parts.append(jnp.sum(wi * jnp.maximum(dots, 0.0), axis=0, keepdims=True))
        sc_ref[0, :, pl.ds(pl.multiple_of(gi * width, width), width)] = jnp.concatenate(parts, axis=1)
        return carry

    lax.fori_loop(0, n_pages // PG, group, 0)
    kin = kin_ref[0].astype(BF16).astype(F32)
    dots = jnp.sum(qb.astype(F32) * kin, axis=1, keepdims=True) * (D_IDX_DIM ** -0.5)
    own = jnp.sum(wi * jnp.maximum(dots, 0.0), axis=0, keepdims=True)
    lane = lax.broadcasted_iota(I32, (1, LANES), 1)
    self_ref[0] = jnp.where(lane == 0, own, -jnp.inf)


def _d_index(page_flat, n_pages, kidx_t, qi3, wi3, kin3):
    nb = qi3.shape[0]
    assert n_pages % PG == 0
    per = lambda shp: pl.BlockSpec((1,) + shp, lambda b, pt: (b, 0, 0))
    gs = pltpu.PrefetchScalarGridSpec(
        num_scalar_prefetch=1,
        grid=(nb,),
        in_specs=[pl.BlockSpec(memory_space=pl.ANY),
                  per((D_IDX_HEADS, D_IDX_DIM)), per((D_IDX_HEADS, 1)), per((1, D_IDX_DIM))],
        out_specs=[per((1, n_pages * PAGE)), per((1, LANES))],
        scratch_shapes=[pltpu.VMEM((2, n_pages, D_IDX_DIM, PAGE), F32),
                        pltpu.SemaphoreType.DMA((1, 2))],
    )
    return pl.pallas_call(
        functools.partial(_d_index_body, n_pages=n_pages),
        grid_spec=gs,
        out_shape=[jax.ShapeDtypeStruct((nb, 1, n_pages * PAGE), F32),
                   jax.ShapeDtypeStruct((nb, 1, LANES), F32)],
        compiler_params=_cp("arbitrary"),
        name="d_index",
    )(page_flat, kidx_t, qi3, wi3, kin3)


def _d_thr_body(sc_ref, thr_ref, bound_ref, key_ref, *, k_top):
    n_chunks = sc_ref.shape[0] // LANES

    def fill(c, carry):
        off = pl.multiple_of(c * LANES, LANES)
        key_ref[pl.ds(off, LANES), :] = _sort_key(sc_ref[pl.ds(off, LANES), :])
        return carry

    lax.fori_loop(0, n_chunks, fill, 0)
    thr, bound = _kth_largest(key_ref, n_chunks // CNT_UNROLL, k_top)
    thr_ref[...] = thr
    bound_ref[...] = bound


def _d_thr(scores_t, k_top):
    nk, nq = scores_t.shape
    return pl.pallas_call(
        functools.partial(_d_thr_body, k_top=k_top),
        grid=(1,),
        in_specs=[pl.BlockSpec((nk, nq), lambda i: (0, 0))],
        out_specs=[pl.BlockSpec((1, nq), lambda i: (0, 0))] * 2,
        out_shape=[jax.ShapeDtypeStruct((1, nq), I32)] * 2,
        scratch_shapes=[pltpu.VMEM((nk, nq), I32)],
        compiler_params=_cp("arbitrary"),
        name="d_thr",
    )(scores_t)


def _d_decode_body(pt_ref, thr_ref, bnd_ref, kv_hbm, sc_ref, self_ref, q_ref, kvn_ref, g_ref, o_ref,
                   kv_buf, sem, *, n_pages):
    slot = _paged_fetch(pt_ref, n_pages, [(kv_hbm, kv_buf)], sem)
    b = pl.program_id(0)
    thr = thr_ref[b]
    bound = bnd_ref[b]
    kw = D_KV_HEADS * D_HD
    rep = D_HEADS // D_KV_HEADS
    hrow = lax.broadcasted_iota(I32, (D_HEADS, kw), 0)
    hcol = lax.broadcasted_iota(I32, (D_HEADS, kw), 1)
    diag = (hcol // D_HD) == (hrow // rep)
    q = q_ref[0]
    qbd = jnp.where(diag, jnp.concatenate([q] * D_KV_HEADS, axis=1), 0.0).astype(BF16)
    gp = PG // 2
    width = gp * PAGE

    def group(gi, carry):
        m_old, l_old, acc = carry
        p0 = gi * gp
        kt = jnp.concatenate([kv_buf[slot, p0 + g, 0] for g in range(gp)], axis=1).astype(BF16)
        vt = jnp.concatenate([kv_buf[slot, p0 + g, 1] for g in range(gp)], axis=1).astype(BF16)
        off = pl.multiple_of(gi * width, width)
        key = _sort_key(sc_ref[0, :, pl.ds(off, width)])
        pos = off + lax.broadcasted_iota(I32, (1, width), 1)
        sel = (key > thr) | ((key == thr) & (pos < bound))
        s = _nn(qbd, kt) * (D_HD ** -0.5) + g_ref[:, pl.ds(off, width)]
        s = jnp.where(sel, s, NEG)
        m_new = jnp.maximum(m_old, jnp.max(s, axis=-1, keepdims=True))
        alpha = jnp.exp(m_old - m_new)
        p = jnp.where(sel, jnp.exp(s - m_new), 0.0)
        return (m_new, alpha * l_old + jnp.sum(p, axis=-1, keepdims=True),
                alpha * acc + _nt(p.astype(BF16), vt))

    init = (jnp.full((D_HEADS, 1), NEG, F32), jnp.zeros((D_HEADS, 1), F32), jnp.zeros((D_HEADS, kw), F32))
    m_o, l_o, acc_o = lax.fori_loop(0, n_pages // gp, group, init)
    past = n_pages * PAGE
    kn = kvn_ref[0, :, :kw].astype(BF16).astype(F32)
    vn = kvn_ref[0, :, kw:].astype(BF16).astype(F32)
    qf = qbd.astype(F32)
    s_self = jnp.sum(qf * kn, axis=1, keepdims=True) * (D_HD ** -0.5) + g_ref[:, past:past + LANES][:, 0:1]
    key_s = _sort_key(self_ref[0][:, 0:1])
    sel_s = (key_s > thr) | ((key_s == thr) & (past < bound))
    s_self = jnp.where(sel_s, s_self, NEG)
    m_n = jnp.maximum(m_o, s_self)
    a = jnp.exp(m_o - m_n)
    p_s = jnp.where(sel_s, jnp.exp(s_self - m_n), 0.0)
    l = a * l_o + p_s
    acc = a * acc_o + p_s.astype(BF16).astype(F32) * vn
    o_full = jnp.where(diag, acc / l, 0.0)
    o_ref[0] = (o_full[:, 0:D_HD] + o_full[:, D_HD:2 * D_HD]
                + o_full[:, 2 * D_HD:3 * D_HD] + o_full[:, 3 * D_HD:]).astype(o_ref.dtype)


def _d_decode(page_flat, thr, bound, n_pages, kv_t, scores3, self3, q3, kvn3, gtab):
    nb = q3.shape[0]
    kw = D_KV_HEADS * D_HD
    assert n_pages % (PG // 2) == 0
    per = lambda shp: pl.BlockSpec((1,) + shp, lambda b, pt, th, bd: (b, 0, 0))
    gs = pltpu.PrefetchScalarGridSpec(
        num_scalar_prefetch=3,
        grid=(nb,),
        in_specs=[pl.BlockSpec(memory_space=pl.ANY),
                  per((1, n_pages * PAGE)), per((1, LANES)), per((D_HEADS, D_HD)), per((1, 2 * kw)),
                  pl.BlockSpec(gtab.shape, lambda b, pt, th, bd: (0, 0))],
        out_specs=per((D_HEADS, D_HD)),
        scratch_shapes=[pltpu.VMEM((2, n_pages, 2, kw, PAGE), F32),
                        pltpu.SemaphoreType.DMA((1, 2))],
    )
    return pl.pallas_call(
        functools.partial(_d_decode_body, n_pages=n_pages),
        grid_spec=gs,
        out_shape=jax.ShapeDtypeStruct((nb, D_HEADS, D_HD), BF16),
        compiler_params=_cp("arbitrary"),
        name="d_decode",
    )(page_flat, thr, bound, kv_t, scores3, self3, q3, kvn3, gtab)


def _d_bias_by_dist(rel_bias, n):
    return rel_bias[:, :D_HEADS][_rel_bucket(jnp.arange(n, dtype=I32))].astype(F32).T


def _mixer_d(xp, xs, bsz, seq, kv_cache, kidx_cache, page_table, rel_bias, g, d_w_in, d_w_out):
    nb = xs.shape[0]
    n_pages = page_table.shape[1]
    past = n_pages * PAGE
    w_in = jnp.pad(d_w_in, ((0, 0), (0, D_HW - d_w_in.shape[1]))).astype(BF16)
    w_out = d_w_out.astype(BF16)
    kw = D_KV_HEADS * D_HD

    h_p = _mm(xp, w_in, g=g)
    h3 = h_p.reshape(bsz, seq, D_HW)
    t = Q_TILE
    n_bank = min(seq // t, -(-(REL_MAX_DIST + t) // t) + 1)
    vec = _d_bias_by_dist(rel_bias, n_bank * t)
    w = jnp.concatenate([vec, jnp.zeros((D_HEADS, t), F32)], axis=1)
    hank = _toeplitz(w, t, n_bank * t)
    bank = jnp.transpose(hank.reshape(D_HEADS, t, n_bank, t), (2, 0, 1, 3))
    wit = jnp.transpose(h3[:, :, D_WI0:D_WI0 + D_IDX_HEADS], (0, 2, 1))
    kb = h3[:, :, D_KV0:D_KV0 + kw].astype(BF16)
    vt = jnp.transpose(h3[:, :, D_KV0 + kw:D_QI0], (0, 2, 1)).astype(BF16)
    qir = jnp.transpose(h3[:, :, D_QI0:D_KI0].reshape(bsz, seq // D_TQ, D_TQ, D_IDX_HEADS, D_IDX_DIM),
                        (0, 1, 3, 2, 4)).reshape(bsz, seq * D_IDX_HEADS, D_IDX_DIM).astype(BF16)
    ot = _d_prompt(h3, qir, wit, kb, vt, bank, min(D_TOPK_MAX, seq // 4))
    xp = _mm(jnp.transpose(ot, (0, 2, 1)).reshape(bsz * seq, -1), w_out, res=xp)
    kv_p = h3[:, :, D_KV0:D_QI0].reshape(bsz, seq, 2, D_KV_HEADS, D_HD)
    kidx_p = h3[:, :, D_KI0:D_WI0]

    h_s = _mm(xs, w_in, g=g)
    page_flat = page_table.reshape(-1)
    qi3 = h_s[:, D_QI0:D_KI0].reshape(nb, D_IDX_HEADS, D_IDX_DIM)
    wi3 = h_s[:, D_WI0:D_WI0 + D_IDX_HEADS].reshape(nb, D_IDX_HEADS, 1)
    kin3 = h_s[:, D_KI0:D_WI0].reshape(nb, 1, D_IDX_DIM)
    sc3, self3 = _d_index(page_flat, n_pages, jnp.transpose(kidx_cache, (0, 2, 1)), qi3, wi3, kin3)
    n_rows = -(-(past + LANES) // (CNT_UNROLL * LANES)) * (CNT_UNROLL * LANES)
    scores_t = jnp.concatenate([sc3.reshape(nb, past), self3.reshape(nb, LANES),
                                jnp.full((nb, n_rows - past - LANES), -jnp.inf, F32)], axis=1).T
    thr, bound = _d_thr(scores_t, min(D_TOPK_MAX, (past + 1) // 4))
    vec_s = _d_bias_by_dist(rel_bias, past + 1)
    gtab_s = jnp.concatenate([vec_s[:, :0:-1], jnp.broadcast_to(vec_s[:, 0:1], (D_HEADS, LANES))], axis=1)
    kv_t = jnp.transpose(kv_cache, (0, 2, 3, 4, 1)).reshape(kv_cache.shape[0], 2, kw, PAGE)
    o_s = _d_decode(page_flat, thr[0], bound[0], n_pages, kv_t, sc3, self3,
                    h_s[:, :D_KV0].reshape(nb, D_HEADS, D_HD), h_s[:, D_KV0:D_QI0].reshape(nb, 1, 2 * kw), gtab_s)
    xs = _mm(o_s.reshape(nb, -1), w_out, res=xs)
    kv_s = h_s[:, D_KV0:D_QI0].reshape(nb, 1, 2, D_KV_HEADS, D_HD)
    kidx_s = h_s[:, D_KI0:D_WI0].reshape(nb, 1, D_IDX_DIM)
    return xp, xs, kv_p, kidx_p, kv_s, kidx_s


def kernel(x_prompt, x_sample, mem_prompt, cache_a1_kv, cache_a2_kv, cache_a3_kv, state_b_conv,
           cache_c_latent, cache_c_krope, cache_d_kv, cache_d_kidx, cache_mem_kv, page_table,
           rel_bias, g_mix, g_cross, g_ffn, g_final, w_xq, w_xkv, w_xo, w_ffn_in, w_ffn_out,
           a_w_in, a_w_out, b_w_pw1, b_b_pw1, b_w_dw, b_b_dw, b_ln_g, b_ln_b, b_w_pw2, b_b_pw2,
           c_w_down, c_g_q, c_g_kv, c_w_uq, c_w_uk, c_w_uv, c_w_out, d_w_in, d_w_out):
    bsz, seq, d = x_prompt.shape
    nb = x_sample.shape[0]
    assert x_sample.shape[1] == 1
    depth = g_mix.shape[0]
    xp = x_prompt.reshape(bsz * seq, d)
    xs = x_sample.reshape(nb, d)
    mem2 = mem_prompt.reshape(bsz * MEM_LEN, d)
    hw = X_HEADS * X_HD
    mem5 = cache_mem_kv.reshape(depth, nb, MEM_LEN, 2 * X_HEADS, X_HD)
    mem_kv_out = []
    outs = {}
    for i in range(depth):
        kind = i % 4
        if kind == 0:
            xp, xs, a_p, a_s = _mixer_a(xp, xs, bsz, seq, [cache_a1_kv, cache_a2_kv, cache_a3_kv],
                                        rel_bias, g_mix[i], a_w_in, a_w_out)
            outs["a_p"], outs["a_s"] = a_p, a_s
        elif kind == 1:
            xp, xs, conv_p, conv_s = _mixer_b(xp, xs, bsz, seq, state_b_conv, g_mix[i], b_w_pw1, b_b_pw1,
                                              b_w_dw, b_b_dw, b_ln_g, b_ln_b, b_w_pw2, b_b_pw2)
            outs["conv"] = (conv_p, conv_s)
        elif kind == 2:
            xp, xs, lat_p, kr_p, lat_s, kr_s = _mixer_c(xp, xs, bsz, seq, cache_c_latent, cache_c_krope,
                                                        page_table, g_mix[i], c_w_down, c_g_q, c_g_kv,
                                                        c_w_uq, c_w_uk, c_w_uv, c_w_out)
            outs["c"] = (lat_p, kr_p, lat_s, kr_s)
        else:
            xp, xs, kv_p, kidx_p, kv_s, kidx_s = _mixer_d(xp, xs, bsz, seq, cache_d_kv, cache_d_kidx,
                                                          page_table, rel_bias, g_mix[i], d_w_in, d_w_out)
            outs["d"] = (kv_p, kidx_p, kv_s, kidx_s)
        mkv = _mm(mem2, w_xkv[i].astype(BF16))
        mem_kv_out.append(mkv.reshape(bsz, MEM_LEN, 2, X_HEADS, X_HD))
        wq = w_xq[i].astype(BF16)
        wo = w_xo[i].astype(BF16)
        xp = _cross(xp.reshape(bsz, seq, d), g_cross[i], wq, wo, mkv.reshape(bsz, MEM_LEN, 2 * hw)).reshape(bsz * seq, d)
        q_s = _mm(xs, wq, g=g_cross[i]).reshape(nb, X_HEADS, X_HD)
        xs = _mm(_cross_s(q_s, mem5, i).reshape(nb, hw), wo, res=xs)
        w_in = w_ffn_in[i].astype(BF16)
        w_out = w_ffn_out[i].astype(BF16)
        xp = _ffn(xp, g_ffn[i], w_in, w_out)
        xs = _ffn(xs, g_ffn[i], w_in, w_out)
    y_p = _rmsnorm(xp, g_final).reshape(bsz, seq, d)
    y_s = _rmsnorm(xs, g_final).reshape(nb, 1, d)
    a_p, a_s = outs["a_p"], outs["a_s"]
    conv_p, conv_s = outs["conv"]
    lat_p, kr_p, lat_s, kr_s = outs["c"]
    kv_p, kidx_p, kv_s, kidx_s = outs["d"]
    return (y_p, y_s, a_p[0], a_p[1], a_p[2], a_s[0], a_s[1], a_s[2], conv_p, conv_s,
            lat_p, kr_p, lat_s, kr_s, kv_p, kidx_p, kv_s, kidx_s, jnp.stack(mem_kv_out))
```

```python
import functools
import math

import jax
import jax.numpy as jnp
import numpy as np
from jax import lax
from jax.experimental import pallas as pl
from jax.experimental.pallas import tpu as pltpu

F32 = jnp.float32
BF16 = jnp.bfloat16
I32 = jnp.int32

EPS = 1e-6
PAGE = 128
LANES = 128
VMEM_LIMIT = 52 * 1024 * 1024
NEG = -1e30
M_INIT = -5e29

REL_BUCKETS = 32
REL_MAX_DIST = 2048
A_WINDOWS = (128, 512, 2048)
A_DILATIONS = (1, 4, 16)
A_HEADS = 8
A_HD = 64
A_NKEYS = 129
B_CONV_WIDTH = 31
C_HEADS = 16
C_Q_LORA = 384
C_KV_LORA = 256
C_NOPE = 64
C_ROPE = 32
C_V = 64
ROPE_THETA = 10000.0
D_HEADS = 16
D_KV_HEADS = 4
D_HD = 64
D_IDX_HEADS = 8
D_IDX_DIM = 64
D_TOPK_MAX = 256
X_HEADS = 4
X_HD = 128
MEM_LEN = 256
Q_TILE = 128


def _cp(*sem):
    return pltpu.CompilerParams(dimension_semantics=sem, vmem_limit_bytes=VMEM_LIMIT)


def _nt(a, b):
    return lax.dot_general(a, b, (((1,), (1,)), ((), ())), preferred_element_type=F32)


def _nn(a, b):
    return jnp.dot(a, b, preferred_element_type=F32)


def _sigmoid(x):
    return 1.0 / (1.0 + jnp.exp(-x))


def _rms(x, g):
    return x * lax.rsqrt(jnp.mean(x * x, axis=-1, keepdims=True) + EPS) * g


def _rel_bucket(dist):
    n = jnp.maximum(dist, 0)
    max_exact = REL_BUCKETS // 2
    nf = jnp.maximum(n, 1).astype(F32)
    large = max_exact + (jnp.log(nf / max_exact) / math.log(REL_MAX_DIST / max_exact)
                         * (REL_BUCKETS - max_exact)).astype(I32)
    large = jnp.minimum(large, REL_BUCKETS - 1)
    return jnp.where(n < max_exact, n, large)


def _mm_body(*refs, norm, bias, glu, resid):
    it = iter(refs)
    x_ref = next(it)
    w_ref = next(it)
    g_ref = next(it) if norm else None
    b_ref = next(it) if bias else None
    w2_ref = next(it) if glu else None
    b2_ref = next(it) if (glu and bias) else None
    r_ref = next(it) if resid else None
    o_ref = next(it)
    xn_ref = next(it)

    @pl.when(pl.program_id(1) == 0)
    def _():
        x = x_ref[...].astype(F32)
        if norm:
            x = _rms(x, g_ref[...])
        xn_ref[...] = x.astype(BF16)

    xn = xn_ref[...]
    h = _nn(xn, w_ref[...])
    if bias:
        h = h + b_ref[...]
    if glu:
        h2 = _nn(xn, w2_ref[...])
        if bias:
            h2 = h2 + b2_ref[...]
        h = h * _sigmoid(h2)
    if resid:
        h = h + r_ref[...]
    o_ref[...] = h.astype(o_ref.dtype)


def _mm(x, w, *, g=None, b=None, w2=None, b2=None, res=None, out_dtype=F32, tm=512, tn=None):
    n, k = x.shape
    m = w.shape[1]
    tm = min(tm, n)
    tn = tn or m
    assert n % tm == 0 and m % tn == 0
    args = [x, w]
    specs = [pl.BlockSpec((tm, k), lambda i, j: (i, 0)),
             pl.BlockSpec((k, tn), lambda i, j: (0, j))]
    if g is not None:
        args.append(g.reshape(1, k))
        specs.append(pl.BlockSpec((1, k), lambda i, j: (0, 0)))
    if b is not None:
        args.append(b.reshape(1, m))
        specs.append(pl.BlockSpec((1, tn), lambda i, j: (0, j)))
    if w2 is not None:
        args.append(w2)
        specs.append(pl.BlockSpec((k, tn), lambda i, j: (0, j)))
        if b2 is not None:
            args.append(b2.reshape(1, m))
            specs.append(pl.BlockSpec((1, tn), lambda i, j: (0, j)))
    if res is not None:
        args.append(res)
        specs.append(pl.BlockSpec((tm, tn), lambda i, j: (i, j)))
    body = functools.partial(_mm_body, norm=g is not None, bias=b is not None,
                             glu=w2 is not None, resid=res is not None)
    return pl.pallas_call(
        body,
        grid=(n // tm, m // tn),
        in_specs=specs,
        out_specs=pl.BlockSpec((tm, tn), lambda i, j: (i, j)),
        out_shape=jax.ShapeDtypeStruct((n, m), out_dtype),
        scratch_shapes=[pltpu.VMEM((tm, k), BF16)],
        compiler_params=_cp("parallel", "arbitrary"),
        name="mm",
    )(*args)


def _ffn_body(x_ref, g_ref, wg_ref, wu_ref, wo_ref, o_ref, xn_ref):
    j = pl.program_id(1)

    @pl.when(j == 0)
    def _():
        x = x_ref[...]
        xn_ref[...] = _rms(x, g_ref[...]).astype(BF16)
        o_ref[...] = x

    xn = xn_ref[...]
    hg = _nn(xn, wg_ref[...])
    hu = _nn(xn, wu_ref[...])
    a = (hg * _sigmoid(hg) * hu).astype(BF16)
    o_ref[...] += _nn(a, wo_ref[...])


def _ffn(x, g, w_in, w_out, *, tm=512):
    n, d = x.shape
    dff = w_out.shape[0]
    tf = dff // 2
    assert tf % LANES == 0
    tm = min(tm, n)
    nf = dff // tf
    return pl.pallas_call(
        _ffn_body,
        grid=(n // tm, nf),
        in_specs=[pl.BlockSpec((tm, d), lambda i, j: (i, 0)),
                  pl.BlockSpec((1, d), lambda i, j: (0, 0)),
                  pl.BlockSpec((d, tf), lambda i, j: (0, j)),
                  pl.BlockSpec((d, tf), lambda i, j: (0, j + nf)),
                  pl.BlockSpec((tf, d), lambda i, j: (j, 0))],
        out_specs=pl.BlockSpec((tm, d), lambda i, j: (i, 0)),
        out_shape=jax.ShapeDtypeStruct((n, d), F32),
        scratch_shapes=[pltpu.VMEM((tm, d), BF16)],
        compiler_params=_cp("parallel", "arbitrary"),
        name="ffn",
    )(x, g.reshape(1, d), w_in, w_in, w_out)


def _cross_body(x_ref, g_ref, wq_ref, wo_ref, kv_ref, o_ref):
    x = x_ref[0]
    xn = _rms(x, g_ref[...]).astype(BF16)
    q = _nn(xn, wq_ref[...]).astype(BF16)
    hw = X_HEADS * X_HD
    outs = []
    for h in range(X_HEADS):
        kh = kv_ref[0, :, h * X_HD:(h + 1) * X_HD].astype(BF16)
        vh = kv_ref[0, :, hw + h * X_HD:hw + (h + 1) * X_HD].astype(BF16)
        s = _nt(q[:, h * X_HD:(h + 1) * X_HD], kh) * (X_HD ** -0.5)
        m = jnp.max(s, axis=-1, keepdims=True)
        p = jnp.exp(s - m)
        l = jnp.sum(p, axis=-1, keepdims=True)
        outs.append(_nn((p / l).astype(BF16), vh))
    o = jnp.concatenate(outs, axis=-1).astype(BF16)
    o_ref[0] = x + _nn(o, wo_ref[...])


def _cross(x3, g, wq, wo, kv3, *, tq=512):
    bsz, t, d = x3.shape
    tq = min(tq, t)
    hw = X_HEADS * X_HD
    return pl.pallas_call(
        _cross_body,
        grid=(bsz, t // tq),
        in_specs=[pl.BlockSpec((1, tq, d), lambda b, i: (b, i, 0)),
                  pl.BlockSpec((1, d), lambda b, i: (0, 0)),
                  pl.BlockSpec((d, hw), lambda b, i: (0, 0)),
                  pl.BlockSpec((hw, d), lambda b, i: (0, 0)),
                  pl.BlockSpec((1, MEM_LEN, 2 * hw), lambda b, i: (b, 0, 0))],
        out_specs=pl.BlockSpec((1, tq, d), lambda b, i: (b, i, 0)),
        out_shape=jax.ShapeDtypeStruct((bsz, t, d), F32),
        compiler_params=_cp("parallel", "parallel"),
        name="cross",
    )(x3, g.reshape(1, d), wq, wo, kv3)


def _cross_s_body(q_ref, kv_ref, o_ref):
    def one(j, carry):
        k = kv_ref[0, j, :, 0:X_HEADS, :]
        v = kv_ref[0, j, :, X_HEADS:2 * X_HEADS, :]
        q = q_ref[j]
        s = jnp.sum(k * q[None], axis=-1, keepdims=True) * (X_HD ** -0.5)
        m = jnp.max(s, axis=0, keepdims=True)
        p = jnp.exp(s - m)
        l = jnp.sum(p, axis=0, keepdims=True)
        o_ref[j] = jnp.sum((p / l) * v, axis=0)
        return carry

    lax.fori_loop(0, q_ref.shape[0], one, 0)


def _cross_s(q3, mem_kv5, layer, *, tb=8):
    nb = q3.shape[0]
    tb = min(tb, nb)
    return pl.pallas_call(
        _cross_s_body,
        grid=(nb // tb,),
        in_specs=[pl.BlockSpec((tb, X_HEADS, X_HD), lambda b: (b, 0, 0)),
                  pl.BlockSpec((1, tb, MEM_LEN, 2 * X_HEADS, X_HD), lambda b: (layer, b, 0, 0, 0))],
        out_specs=pl.BlockSpec((tb, X_HEADS, X_HD), lambda b: (b, 0, 0)),
        out_shape=jax.ShapeDtypeStruct((nb, X_HEADS, X_HD), F32),
        compiler_params=_cp("parallel"),
        name="cross_s",
    )(q3, mem_kv5)


def _rms_body(x_ref, g_ref, o_ref):
    o_ref[...] = _rms(x_ref[...], g_ref[...])


def _rmsnorm(x, g, *, tm=512):
    n, d = x.shape
    tm = min(tm, n)
    return pl.pallas_call(
        _rms_body,
        grid=(n // tm,),
        in_specs=[pl.BlockSpec((tm, d), lambda i: (i, 0)),
                  pl.BlockSpec((1, d), lambda i: (0, 0))],
        out_specs=pl.BlockSpec((tm, d), lambda i: (i, 0)),
        out_shape=jax.ShapeDtypeStruct((n, d), F32),
        compiler_params=_cp("parallel"),
        name="rmsnorm",
    )(x, g.reshape(1, d))


def _a_bias_vec(rel_bias, g):
    dist = A_DILATIONS[g] * jnp.arange(A_NKEYS, dtype=I32)
    return rel_bias[_rel_bucket(dist)][:, g * A_HEADS:(g + 1) * A_HEADS].astype(F32).T


def _a_prompt_body(q_ref, kc_ref, kp_ref, vc_ref, vp_ref, bias_ref, o_ref, lse_ref):
    j = pl.program_id(1)
    q = q_ref[0].astype(BF16)
    k = jnp.concatenate([kp_ref[0], kc_ref[0]], axis=0).astype(BF16)
    v = jnp.concatenate([vp_ref[0], vc_ref[0]], axis=0).astype(BF16)
    t = Q_TILE
    row = lax.broadcasted_iota(I32, (t, 2 * t), 0)
    col = lax.broadcasted_iota(I32, (t, 2 * t), 1)
    back = row + t - col
    ok = (back >= 0) & (back <= t) & ((col >= t) | (j > 0))
    outs, lses = [], []
    for h in range(A_HEADS):
        sl = slice(h * A_HD, (h + 1) * A_HD)
        s = _nt(q[:, sl], k[:, sl]) * (A_HD ** -0.5) + bias_ref[h]
        s = jnp.where(ok, s, NEG)
        m = jnp.max(s, axis=-1, keepdims=True)
        p = jnp.exp(s - m)
        l = jnp.sum(p, axis=-1, keepdims=True)
        outs.append(_nn(p.astype(BF16), v[:, sl]) / l)
        lses.append(jnp.broadcast_to(m + jnp.log(l), (t, A_HD)))
    o_ref[0] = jnp.concatenate(outs, axis=-1)
    lse_ref[0] = jnp.concatenate(lses, axis=-1)


def _a_prompt_group(qkv_g, bias_tile):
    nr, sj, _ = qkv_g.shape
    hw = A_HEADS * A_HD
    t = Q_TILE
    assert sj % t == 0

    def cur(which):
        return pl.BlockSpec((1, t, hw), lambda r, j: (r, j, which))

    def prev(which):
        return pl.BlockSpec((1, t, hw), lambda r, j: (r, jnp.maximum(j - 1, 0), which))

    out_spec = pl.BlockSpec((1, t, hw), lambda r, j: (r, j, 0))
    return pl.pallas_call(
        _a_prompt_body,
        grid=(nr, sj // t),
        in_specs=[cur(0), cur(1), prev(1), cur(2), prev(2),
                  pl.BlockSpec((A_HEADS, t, 2 * t), lambda r, j: (0, 0, 0))],
        out_specs=[out_spec, out_spec],
        out_shape=[jax.ShapeDtypeStruct((nr, sj, hw), F32)] * 2,
        compiler_params=_cp("parallel", "parallel"),
        name="a_prompt",
    )(qkv_g, qkv_g, qkv_g, qkv_g, qkv_g, bias_tile)


def _toeplitz(w, n_rows, n_cols):
    h, p = w.shape
    assert n_cols <= p - 1
    flat = jnp.tile(w, (1, n_rows))[:, :n_rows * (p - 1)]
    return flat.reshape(h, n_rows, p - 1)[:, :, :n_cols]


def _a_bias_tile(vec):
    t = Q_TILE
    nh = vec.shape[0]
    w = jnp.concatenate([vec[:, t:t + 1], jnp.zeros((nh, 2 * t - 1), F32), vec[:, :t]], axis=1)
    return jnp.transpose(_toeplitz(w, 2 * t, t), (0, 2, 1))


def _a_combine_body(o0, o1, o2, l0, l1, l2, x_ref, w_ref, out_ref):
    a0, a1, a2 = l0[...], l1[...], l2[...]
    m = jnp.maximum(jnp.maximum(a0, a1), a2)
    e0, e1, e2 = jnp.exp(a0 - m), jnp.exp(a1 - m), jnp.exp(a2 - m)
    den = e0 + e1 + e2
    o = (e0 / den) * o0[...] + (e1 / den) * o1[...] + (e2 / den) * o2[...]
    out_ref[...] = x_ref[...] + _nn(o.astype(BF16), w_ref[...])


def _a_combine(os_, ls_, x, w_out, *, tm=512):
    n, d = x.shape
    hw = w_out.shape[0]
    tm = min(tm, n)
    small = pl.BlockSpec((tm, hw), lambda i: (i, 0))
    return pl.pallas_call(
        _a_combine_body,
        grid=(n // tm,),
        in_specs=[small] * 6 + [pl.BlockSpec((tm, d), lambda i: (i, 0)),
                                pl.BlockSpec((hw, d), lambda i: (0, 0))],
        out_specs=pl.BlockSpec((tm, d), lambda i: (i, 0)),
        out_shape=jax.ShapeDtypeStruct((n, d), F32),
        compiler_params=_cp("parallel"),
        name="a_combine",
    )(*os_, *ls_, x, w_out)


def _row_to_col(r):
    n = r.shape[1]
    eye = (lax.broadcasted_iota(I32, (LANES, LANES), 0) == lax.broadcasted_iota(I32, (LANES, LANES), 1))
    cols = []
    for c in range(n // LANES):
        blk = jnp.broadcast_to(r[:, c * LANES:(c + 1) * LANES], (LANES, LANES))
        cols.append(jnp.sum(jnp.where(eye, blk, 0.0), axis=1, keepdims=True))
    return jnp.concatenate(cols, axis=0)


def _col_to_row(c):
    n = c.shape[0]
    eye = (lax.broadcasted_iota(I32, (LANES, LANES), 0) == lax.broadcasted_iota(I32, (LANES, LANES), 1))
    rows = []
    for i in range(n // LANES):
        blk = jnp.broadcast_to(c[i * LANES:(i + 1) * LANES, :], (LANES, LANES))
        rows.append(jnp.sum(jnp.where(eye, blk, 0.0), axis=0, keepdims=True))
    return jnp.concatenate(rows, axis=1)


def _a_sample_body(q_ref, kvn_ref, bias_ref, c_ref, o_ref, lse_ref, cn_ref, *, width):
    hw = A_HEADS * A_HD
    q_col = _row_to_col(q_ref[0])
    kvn_row = kvn_ref[0]
    kvn_col = _row_to_col(kvn_row)
    scale = A_HD ** -0.5
    s_rows, self_rows = [], []
    for h in range(A_HEADS):
        sl = slice(h * A_HD, (h + 1) * A_HD)
        kt = c_ref[0, sl, :]
        s_rows.append(jnp.sum(kt * q_col[sl], axis=0, keepdims=True))
        self_rows.append(jnp.sum(kvn_col[sl] * q_col[sl], axis=0, keepdims=True))
    bias = bias_ref[...]
    s = jnp.concatenate(s_rows, axis=0) * scale + bias[:, :width]
    s_self = jnp.concatenate(self_rows, axis=0) * scale + bias[:, width:width + 1]
    m = jnp.maximum(jnp.max(s, axis=-1, keepdims=True), s_self)
    p = jnp.exp(s - m)
    p_self = jnp.exp(s_self - m)
    l = jnp.sum(p, axis=-1, keepdims=True) + p_self
    lse = m + jnp.log(l)
    o_cols, lse_cols = [], []
    for h in range(A_HEADS):
        sl = slice(hw + h * A_HD, hw + (h + 1) * A_HD)
        vt = c_ref[0, sl, :]
        oc = jnp.sum(vt * p[h:h + 1, :], axis=1, keepdims=True) + kvn_col[sl] * p_self[h:h + 1, :]
        o_cols.append(oc / l[h:h + 1, :])
        lse_cols.append(jnp.broadcast_to(lse[h:h + 1, :], (A_HD, 1)))
    o_ref[0] = _col_to_row(jnp.concatenate(o_cols, axis=0))
    lse_ref[0] = _col_to_row(jnp.concatenate(lse_cols, axis=0))
    rc = 128
    lane = lax.broadcasted_iota(I32, (rc, width), 1)
    for r0 in range(0, 2 * hw, rc):
        rolled = pltpu.roll(c_ref[0, r0:r0 + rc, :], width - 1, 1)
        cn_ref[0, r0:r0 + rc, :] = jnp.where(lane == width - 1, kvn_col[r0:r0 + rc], rolled)


def _a_sample_group(q_g, kvn_g, cache_t, bias_s):
    bsz, rows, width = cache_t.shape
    hw = A_HEADS * A_HD
    body = functools.partial(_a_sample_body, width=width)
    return pl.pallas_call(
        body,
        grid=(bsz,),
        in_specs=[pl.BlockSpec((1, 1, hw), lambda b: (b, 0, 0)),
                  pl.BlockSpec((1, 1, 2 * hw), lambda b: (b, 0, 0)),
                  pl.BlockSpec((A_HEADS, width + LANES), lambda b: (0, 0)),
                  pl.BlockSpec((1, rows, width), lambda b: (b, 0, 0))],
        out_specs=[pl.BlockSpec((1, 1, hw), lambda b: (b, 0, 0)),
                   pl.BlockSpec((1, 1, hw), lambda b: (b, 0, 0)),
                   pl.BlockSpec((1, rows, width), lambda b: (b, 0, 0))],
        out_shape=[jax.ShapeDtypeStruct((bsz, 1, hw), F32),
                   jax.ShapeDtypeStruct((bsz, 1, hw), F32),
                   jax.ShapeDtypeStruct((bsz, rows, width), F32)],
        compiler_params=_cp("parallel"),
        name="a_sample",
    )(q_g, kvn_g, bias_s, cache_t)


def _mixer_a(xp, xs, bsz, seq, caches, rel_bias, g, a_w_in, a_w_out):
    hw = A_HEADS * A_HD
    w_in = a_w_in.astype(BF16)
    w_out = a_w_out.astype(BF16)
    qkv_s = _mm(xs, w_in, g=g, tn=w_in.shape[1] // 2)
    os_p, ls_p, os_s, ls_s, new_p, new_s = [], [], [], [], [], []
    nb = xs.shape[0]
    dm = xp.shape[1]
    for gi in range(3):
        vec = _a_bias_vec(rel_bias, gi)
        d = A_DILATIONS[gi]
        sj = seq // d
        x_g = jnp.transpose(xp.reshape(bsz, sj, d, dm), (0, 2, 1, 3)).reshape(bsz * seq, dm) if d > 1 else xp
        w_g = jnp.concatenate([w_in[:, (3 * j + gi) * hw:(3 * j + gi + 1) * hw] for j in range(3)], axis=1)
        qkv_g = _mm(x_g, w_g, g=g).reshape(bsz * d, sj, 3 * hw)
        o, lse = _a_prompt_group(qkv_g, _a_bias_tile(vec))

        def natural(a):
            return jnp.transpose(a.reshape(bsz, d, sj, -1), (0, 2, 1, 3)).reshape(bsz * seq, -1)

        os_p.append(natural(o))
        ls_p.append(natural(lse))
        w = min(A_WINDOWS[gi], seq)
        assert w % d == 0
        kv_tail = qkv_g.reshape(bsz, d, sj, 3 * hw)[:, :, sj - w // d:, hw:]
        new_p.append(jnp.transpose(kv_tail, (0, 2, 1, 3)).reshape(bsz, w, 2, A_HEADS, A_HD))
        cache = caches[gi]
        width = cache.shape[1]
        assert width == (A_NKEYS - 1) * d
        cache_t = jnp.transpose(cache, (0, 2, 3, 4, 1)).reshape(nb, 2 * hw, width)
        q_s = qkv_s[:, gi * hw:(gi + 1) * hw].reshape(nb, 1, hw)
        kvn = jnp.concatenate([qkv_s[:, (3 + gi) * hw:(4 + gi) * hw],
                               qkv_s[:, (6 + gi) * hw:(7 + gi) * hw]], axis=1).reshape(nb, 1, 2 * hw)
        lane = jnp.arange(width)
        kk = (width - lane) // d
        bias_c = jnp.where((lane % d == 0)[None, :], vec[:, jnp.clip(kk, 0, A_NKEYS - 1)], NEG)
        bias_s = jnp.concatenate([bias_c, jnp.broadcast_to(vec[:, 0:1], (A_HEADS, LANES))], axis=1)
        o_s, lse_s, cache_new = _a_sample_group(q_s, kvn, cache_t, bias_s)
        os_s.append(o_s.reshape(nb, hw))
        ls_s.append(lse_s.reshape(nb, hw))
        new_s.append(jnp.transpose(cache_new.reshape(nb, 2, A_HEADS, A_HD, width), (0, 4, 1, 2, 3)))
    xp = _a_combine(os_p, ls_p, xp, w_out)
    xs = _a_combine(os_s, ls_s, xs, w_out)
    return xp, xs, new_p, new_s


def _ln_silu_proj(c, lng_ref, lnb_ref, w2_ref, b2_ref, x):
    mu = jnp.mean(c, axis=-1, keepdims=True)
    xc = c - mu
    y = xc * lax.rsqrt(jnp.mean(xc * xc, axis=-1, keepdims=True) + EPS)
    y = y * lng_ref[...] + lnb_ref[...]
    y = y * _sigmoid(y)
    return x + _nn(y.astype(BF16), w2_ref[...]) + b2_ref[...]


def _b_prompt_body(uc_ref, up_ref, x_ref, wdw_ref, bdw_ref, lng_ref, lnb_ref, w2_ref, b2_ref,
                   o_ref, buf_ref, c_ref, *, tq):
    i = pl.program_id(1)
    pad = 32
    buf_ref[0:pad, :] = jnp.where(i > 0, up_ref[0], 0.0)
    buf_ref[pad:pad + tq, :] = uc_ref[0]
    off = pad - (B_CONV_WIDTH - 1)
    rc, cc = 128, 256
    win = rc + pad
    d = uc_ref.shape[2]
    for r0 in range(0, tq, rc):
        for c0 in range(0, d, cc):
            x = buf_ref[r0:r0 + win, c0:c0 + cc]
            acc = jnp.zeros((rc, cc), F32)
            for o in range(8):
                xo = pltpu.roll(x, (win - o) % win, 0) if o else x
                for w in range(B_CONV_WIDTH):
                    if (off + w) % 8 == o:
                        a0 = off + w - o
                        acc = acc + xo[a0:a0 + rc] * wdw_ref[w:w + 1, c0:c0 + cc]
            c_ref[r0:r0 + rc, c0:c0 + cc] = acc
    c = c_ref[...] + bdw_ref[...]
    o_ref[0] = _ln_silu_proj(c, lng_ref, lnb_ref, w2_ref, b2_ref, x_ref[0])


def _b_prompt(u3, x3, wdw, bdw, lng, lnb, w2, b2, *, tq=256):
    bsz, s, d = u3.shape
    tq = min(tq, s)
    pad = 32
    body = functools.partial(_b_prompt_body, tq=tq)
    vec = lambda: pl.BlockSpec((1, d), lambda b, i: (0, 0))
    return pl.pallas_call(
        body,
        grid=(bsz, s // tq),
        in_specs=[pl.BlockSpec((1, tq, d), lambda b, i: (b, i, 0)),
                  pl.BlockSpec((1, pad, d), lambda b, i: (b, jnp.maximum(i * (tq // pad) - 1, 0), 0)),
                  pl.BlockSpec((1, tq, d), lambda b, i: (b, i, 0)),
                  pl.BlockSpec((B_CONV_WIDTH, d), lambda b, i: (0, 0)),
                  vec(), vec(), vec(),
                  pl.BlockSpec((d, d), lambda b, i: (0, 0)),
                  vec()],
        out_specs=pl.BlockSpec((1, tq, d), lambda b, i: (b, i, 0)),
        out_shape=jax.ShapeDtypeStruct((bsz, s, d), F32),
        scratch_shapes=[pltpu.VMEM((tq + pad, d), F32), pltpu.VMEM((tq, d), F32)],
        compiler_params=_cp("parallel", "parallel"),
        name="b_prompt",
    )(u3, u3, x3, wdw, bdw.reshape(1, d), lng.reshape(1, d), lnb.reshape(1, d), w2, b2.reshape(1, d))


def _b_sample_body(st_ref, u_ref, x_ref, wdw_ref, bdw_ref, lng_ref, lnb_ref, w2_ref, b2_ref,
                   o_ref, ns_ref):
    nw = B_CONV_WIDTH - 1
    u = u_ref[...]
    acc = u * wdw_ref[nw:nw + 1, :] + bdw_ref[...]
    for w in range(nw):
        acc = acc + st_ref[w] * wdw_ref[w:w + 1, :]
        if w > 0:
            ns_ref[w - 1] = st_ref[w]
    ns_ref[nw - 1] = u
    o_ref[...] = _ln_silu_proj(acc, lng_ref, lnb_ref, w2_ref, b2_ref, x_ref[...])


def _b_sample(state_t, u, x, wdw, bdw, lng, lnb, w2, b2, *, tb=32):
    nw, nb, d = state_t.shape
    tb = min(tb, nb)
    vec = lambda: pl.BlockSpec((1, d), lambda i: (0, 0))
    return pl.pallas_call(
        _b_sample_body,
        grid=(nb // tb,),
        in_specs=[pl.BlockSpec((nw, tb, d), lambda i: (0, i, 0)),
                  pl.BlockSpec((tb, d), lambda i: (i, 0)),
                  pl.BlockSpec((tb, d), lambda i: (i, 0)),
                  pl.BlockSpec((B_CONV_WIDTH, d), lambda i: (0, 0)),
                  vec(), vec(), vec(),
                  pl.BlockSpec((d, d), lambda i: (0, 0)),
                  vec()],
        out_specs=[pl.BlockSpec((tb, d), lambda i: (i, 0)),
                   pl.BlockSpec((nw, tb, d), lambda i: (0, i, 0))],
        out_shape=[jax.ShapeDtypeStruct((nb, d), F32),
                   jax.ShapeDtypeStruct((nw, nb, d), F32)],
        compiler_params=_cp("parallel"),
        name="b_sample",
    )(state_t, u, x, wdw, bdw.reshape(1, d), lng.reshape(1, d), lnb.reshape(1, d), w2, b2.reshape(1, d))


def _mixer_b(xp, xs, bsz, seq, state, g, w_pw1, b_pw1, w_dw, b_dw, ln_g, ln_b, w_pw2, b_pw2):
    d = xp.shape[1]
    wa = w_pw1[:, :d].astype(BF16)
    wb = w_pw1[:, d:].astype(BF16)
    w2 = w_pw2.astype(BF16)
    nw = B_CONV_WIDTH - 1
    u_p = _mm(xp, wa, g=g, b=b_pw1[:d], w2=wb, b2=b_pw1[d:])
    u_s = _mm(xs, wa, g=g, b=b_pw1[:d], w2=wb, b2=b_pw1[d:])
    u3 = u_p.reshape(bsz, seq, d)
    xp = _b_prompt(u3, xp.reshape(bsz, seq, d), w_dw, b_dw, ln_g, ln_b, w2, b_pw2).reshape(bsz * seq, d)
    conv_p = u3[:, seq - nw:]
    xs, ns_t = _b_sample(jnp.transpose(state, (1, 0, 2)), u_s, xs, w_dw, b_dw, ln_g, ln_b, w2, b_pw2)
    return xp, xs, conv_p, jnp.transpose(ns_t, (1, 0, 2))


C_QW = 128
C_DOWN = C_Q_LORA + C_KV_LORA


def _c_prep_body(h_ref, ct_ref, st_ref, gq_ref, gkv_ref, wqa_ref, wqb_ref, wuk_ref, wuv_ref,
                 q_ref, k_ref, v_ref, lat_ref, kr_ref):
    h = h_ref[...]
    ct = ct_ref[...]
    st = st_ref[...]
    cq = _rms(h[:, :C_Q_LORA], gq_ref[...]).astype(BF16)
    lat = _rms(h[:, C_Q_LORA:C_DOWN], gkv_ref[...])
    lat_ref[...] = lat
    lat_b = lat.astype(BF16)
    krp = h[:, C_DOWN:C_DOWN + C_QW] * ct + h[:, C_DOWN + C_QW:C_DOWN + 2 * C_QW] * st
    kr_ref[...] = krp[:, C_NOPE:C_NOPE + C_ROPE]
    qa = _nn(cq, wqa_ref[...])
    qb = _nn(cq, wqb_ref[...])
    kn = _nn(lat_b, wuk_ref[...])
    for hd in range(C_HEADS):
        sl = slice(hd * C_QW, (hd + 1) * C_QW)
        q_ref[:, sl] = (qa[:, sl] * ct + qb[:, sl] * st).astype(BF16)
        k_ref[:, sl] = (kn[:, sl] + krp).astype(BF16)
    v_ref[...] = _nn(lat_b, wuv_ref[...]).astype(BF16)


def _c_prep(h, ct, st, gq, gkv, wqa, wqb, wuk, wuv, *, tm=512):
    n, hc = h.shape
    tm = min(tm, n)
    nt = ct.shape[0] // tm
    qw = C_HEADS * C_QW
    vw = C_HEADS * C_V
    full = lambda a: pl.BlockSpec(a.shape, lambda i: (0, 0))
    return pl.pallas_call(
        _c_prep_body,
        grid=(n // tm,),
        in_specs=[pl.BlockSpec((tm, hc), lambda i: (i, 0)),
                  pl.BlockSpec((tm, C_QW), lambda i: (i % nt, 0)),
                  pl.BlockSpec((tm, C_QW), lambda i: (i % nt, 0)),
                  full(gq), full(gkv), full(wqa), full(wqb), full(wuk), full(wuv)],
        out_specs=[pl.BlockSpec((tm, qw), lambda i: (i, 0)),
                   pl.BlockSpec((tm, qw), lambda i: (i, 0)),
                   pl.BlockSpec((tm, vw), lambda i: (i, 0)),
                   pl.BlockSpec((tm, C_KV_LORA), lambda i: (i, 0)),
                   pl.BlockSpec((tm, C_ROPE), lambda i: (i, 0))],
        out_shape=[jax.ShapeDtypeStruct((n, qw), BF16),
                   jax.ShapeDtypeStruct((n, qw), BF16),
                   jax.ShapeDtypeStruct((n, vw), BF16),
                   jax.ShapeDtypeStruct((n, C_KV_LORA), F32),
                   jax.ShapeDtypeStruct((n, C_ROPE), F32)],
        compiler_params=_cp("parallel"),
        name="c_prep",
    )(h, ct, st, gq, gkv, wqa, wqb, wuk, wuv)


def _c_flash_body(q_ref, k_ref, vt_ref, ot_ref, m_ref, l_ref, acc_ref, s_ref, p_ref, a_ref, *, tq, scale):
    i = pl.program_id(2)
    t = C_KC
    m_ref[...] = jnp.full(m_ref.shape, NEG, F32)
    l_ref[...] = jnp.zeros(l_ref.shape, F32)
    acc_ref[...] = jnp.zeros(acc_ref.shape, F32)
    q = q_ref[0]
    krow = lax.broadcasted_iota(I32, (t, tq), 0)
    qcol = lax.broadcasted_iota(I32, (t, tq), 1)
    n_full = i * (tq // t)
    scale2 = scale * math.log2(math.e)

    def chunk(c, masked):
        off = pl.multiple_of(c * t, t)
        kc = k_ref[0, pl.ds(off, t), :]
        vtc = vt_ref[0, :, pl.ds(off, t)]
        for hh in range(C_GRP):
            s_ref[hh] = _nt(kc[:, hh * C_QW:(hh + 1) * C_QW], q[:, hh * C_QW:(hh + 1) * C_QW])
        for hh in range(C_GRP):
            s = s_ref[hh] * scale2
            if masked:
                s = jnp.where(off + krow <= i * tq + qcol, s, NEG)
            s_ref[hh] = s
            m_old = m_ref[hh]
            m_new = jnp.maximum(m_old, jnp.max(s, axis=0, keepdims=True))
            a_ref[hh] = jnp.exp2(m_old - m_new)
            m_ref[hh] = m_new
        for hh in range(C_GRP):
            p = jnp.exp2(s_ref[hh] - m_ref[hh])
            l_ref[hh] = a_ref[hh] * l_ref[hh] + jnp.sum(p, axis=0, keepdims=True)
            p_ref[hh] = p.astype(BF16)
        for hh in range(C_GRP):
            rows = slice(hh * C_V, (hh + 1) * C_V)
            acc_ref[rows, :] = a_ref[hh] * acc_ref[rows, :] + _nn(vtc[rows, :], p_ref[hh])

    def full_chunk(c, carry):
        chunk(c, False)
        return carry

    lax.fori_loop(0, n_full, full_chunk, 0)
    for dc in range(tq // t):
        chunk(n_full + dc, True)
    for hh in range(C_GRP):
        rows = slice(hh * C_V, (hh + 1) * C_V)
        ot_ref[0, rows, :] = (acc_ref[rows, :] / l_ref[hh]).astype(ot_ref.dtype)


C_GRP = 8
C_KC = 256


def _c_flash(q3, k3, vt3, *, tq=512):
    bsz, s, _ = q3.shape
    tq = min(tq, s)
    assert tq % C_KC == 0
    ngrp = C_HEADS // C_GRP
    scale = (C_NOPE + C_ROPE) ** -0.5
    body = functools.partial(_c_flash_body, tq=tq, scale=scale)
    return pl.pallas_call(
        body,
        grid=(bsz, ngrp, s // tq),
        in_specs=[pl.BlockSpec((1, tq, C_GRP * C_QW), lambda b, hp, i: (b, i, hp)),
                  pl.BlockSpec((1, s, C_GRP * C_QW), lambda b, hp, i: (b, 0, hp)),
                  pl.BlockSpec((1, C_GRP * C_V, s), lambda b, hp, i: (b, hp, 0))],
        out_specs=pl.BlockSpec((1, C_GRP * C_V, tq), lambda b, hp, i: (b, hp, i)),
        out_shape=jax.ShapeDtypeStruct((bsz, C_HEADS * C_V, s), BF16),
        scratch_shapes=[pltpu.VMEM((C_GRP, 1, tq), F32), pltpu.VMEM((C_GRP, 1, tq), F32),
                        pltpu.VMEM((C_GRP * C_V, tq), F32),
                        pltpu.VMEM((C_GRP, C_KC, tq), F32), pltpu.VMEM((C_GRP, C_KC, tq), BF16),
                        pltpu.VMEM((C_GRP, 1, tq), F32)],
        compiler_params=_cp("parallel", "parallel", "arbitrary"),
        name="c_flash",
    )(q3, k3, vt3)


def _head_mm_body(x_ref, w_ref, o_ref):
    o_ref[...] = _nn(x_ref[...], w_ref[0]).astype(o_ref.dtype)


def _head_mm(x, w3, *, out_dtype=F32):
    n = x.shape[0]
    nh, kin, kout = w3.shape
    return pl.pallas_call(
        _head_mm_body,
        grid=(nh,),
        in_specs=[pl.BlockSpec((n, kin), lambda h: (0, h)),
                  pl.BlockSpec((1, kin, kout), lambda h: (h, 0, 0))],
        out_specs=pl.BlockSpec((n, kout), lambda h: (0, h)),
        out_shape=jax.ShapeDtypeStruct((n, nh * kout), out_dtype),
        compiler_params=_cp("parallel"),
        name="head_mm",
    )(x, w3)


PG = 16


def _paged_fetch(pt_ref, n_pages, pairs, sem):
    b = pl.program_id(0)
    slot = lax.rem(b, 2)

    def page_copy(bb, sl, p, k):
        src, buf = pairs[k]
        return pltpu.make_async_copy(src.at[pt_ref[bb * n_pages + p]], buf.at[sl, p], sem.at[k, sl])

    def issue(bb, sl):
        def body(p, carry):
            for k in range(len(pairs)):
                page_copy(bb, sl, p, k).start()
            return carry
        lax.fori_loop(0, n_pages, body, 0)

    @pl.when(b == 0)
    def _():
        issue(b, slot)

    @pl.when(b + 1 < pl.num_programs(0))
    def _():
        issue(b + 1, 1 - slot)

    def wait_body(p, carry):
        for k in range(len(pairs)):
            page_copy(b, slot, p, k).wait()
        return carry

    lax.fori_loop(0, n_pages, wait_body, 0)
    return slot


def _c_decode_body(pt_ref, lat_hbm, rope_hbm, ql_ref, qr_ref, latn_ref, krn_ref, o_ref,
                   lat_buf, rope_buf, sem, *, n_pages, scale):
    slot = _paged_fetch(pt_ref, n_pages, [(lat_hbm, lat_buf), (rope_hbm, rope_buf)], sem)
    ql = ql_ref[0]
    qr = qr_ref[0]

    def group(gi, carry):
        m_old, l_old, acc = carry
        p0 = gi * PG
        lat = lat_buf[slot, pl.ds(p0, PG)].reshape(PG * PAGE, C_KV_LORA).astype(BF16)
        s_rope = jnp.concatenate([_nn(qr, rope_buf[slot, p0 + g].astype(BF16)) for g in range(PG)], axis=1)
        s = (_nt(ql, lat) + s_rope) * scale
        m_new = jnp.maximum(m_old, jnp.max(s, axis=-1, keepdims=True))
        alpha = jnp.exp(m_old - m_new)
        p = jnp.exp(s - m_new)
        return (m_new, alpha * l_old + jnp.sum(p, axis=-1, keepdims=True),
                alpha * acc + _nn(p.astype(BF16), lat))

    init = (jnp.full((C_HEADS, 1), NEG, F32), jnp.zeros((C_HEADS, 1), F32),
            jnp.zeros((C_HEADS, C_KV_LORA), F32))
    m_o, l_o, acc_o = lax.fori_loop(0, n_pages // PG, group, init)
    latn = latn_ref[0].astype(BF16)
    krn = krn_ref[0].astype(BF16)
    s_self = (jnp.sum(ql.astype(F32) * latn.astype(F32), axis=1, keepdims=True)
              + jnp.sum(qr.astype(F32) * krn.astype(F32), axis=1, keepdims=True)) * scale
    m_n = jnp.maximum(m_o, s_self)
    a = jnp.exp(m_o - m_n)
    p_s = jnp.exp(s_self - m_n)
    l = a * l_o + p_s
    acc = a * acc_o + p_s.astype(BF16).astype(F32) * latn.astype(F32)
    o_ref[0] = (acc / l).astype(o_ref.dtype)


def _c_decode(page_flat, n_pages, lat_cache, rope_cache_t, ql3, qr3, latn3, krn3):
    nb = ql3.shape[0]
    scale = (C_NOPE + C_ROPE) ** -0.5
    assert n_pages % PG == 0
    per = lambda shp: pl.BlockSpec((1,) + shp, lambda b, pt: (b, 0, 0))
    hbm = pl.BlockSpec(memory_space=pl.ANY)
    gs = pltpu.PrefetchScalarGridSpec(
        num_scalar_prefetch=1,
        grid=(nb,),
        in_specs=[hbm, hbm, per((C_HEADS, C_KV_LORA)), per((C_HEADS, C_ROPE)),
                  per((1, C_KV_LORA)), per((1, C_ROPE))],
        out_specs=per((C_HEADS, C_KV_LORA)),
        scratch_shapes=[pltpu.VMEM((2, n_pages, PAGE, C_KV_LORA), F32),
                        pltpu.VMEM((2, n_pages, C_ROPE, PAGE), F32),
                        pltpu.SemaphoreType.DMA((2, 2))],
    )
    return pl.pallas_call(
        functools.partial(_c_decode_body, n_pages=n_pages, scale=scale),
        grid_spec=gs,
        out_shape=jax.ShapeDtypeStruct((nb, C_HEADS, C_KV_LORA), BF16),
        compiler_params=_cp("arbitrary"),
        name="c_decode",
    )(page_flat, lat_cache, rope_cache_t, ql3, qr3, latn3, krn3)


def _rope_tables(pos):
    inv = ROPE_THETA ** (-jnp.arange(0, C_ROPE, 2, dtype=F32) / C_ROPE)
    ang = pos.astype(F32)[:, None] * inv[None, :]
    cos, sin = jnp.cos(ang), jnp.sin(ang)
    n = pos.shape[0]
    ones = jnp.ones((n, C_NOPE), F32)
    zeros = jnp.zeros((n, C_NOPE), F32)
    tail = jnp.zeros((n, C_QW - C_NOPE - C_ROPE), F32)
    ct = jnp.concatenate([ones, cos, cos, tail], axis=1)
    st = jnp.concatenate([zeros, -sin, sin, tail], axis=1)
    return ct, st


def _rope_swap(w):
    half = C_ROPE // 2
    return jnp.concatenate([w[..., half:], w[..., :half]], axis=-1)


def _pad_cols(w, left, total):
    return jnp.pad(w, ((0, 0), (left, total - left - w.shape[1])))


def _mixer_c(xp, xs, bsz, seq, lat_cache, rope_cache, page_table, g,
             c_w_down, c_g_q, c_g_kv, c_w_uq, c_w_uk, c_w_uv, c_w_out):
    nb = xs.shape[0]
    n_pages = page_table.shape[1]
    past = n_pages * PAGE
    w_rope = c_w_down[:, C_DOWN:]
    w_down = jnp.concatenate([c_w_down[:, :C_DOWN], _pad_cols(w_rope, C_NOPE, C_QW),
                              _pad_cols(_rope_swap(w_rope), C_NOPE, C_QW)], axis=1).astype(BF16)
    uq = c_w_uq.reshape(C_Q_LORA, C_HEADS, C_NOPE + C_ROPE)
    zq = jnp.zeros((C_Q_LORA, C_HEADS, C_QW - C_NOPE - C_ROPE), F32)
    wqa = jnp.concatenate([uq, zq], axis=2).reshape(C_Q_LORA, C_HEADS * C_QW).astype(BF16)
    wqb = jnp.concatenate([jnp.zeros((C_Q_LORA, C_HEADS, C_NOPE), F32), _rope_swap(uq[..., C_NOPE:]), zq],
                          axis=2).reshape(C_Q_LORA, C_HEADS * C_QW).astype(BF16)
    wuk = jnp.concatenate([c_w_uk, jnp.zeros((C_KV_LORA, C_HEADS, C_QW - C_NOPE), F32)],
                          axis=2).reshape(C_KV_LORA, C_HEADS * C_QW).astype(BF16)
    wuv = c_w_uv.reshape(C_KV_LORA, C_HEADS * C_V).astype(BF16)
    w_out = c_w_out.astype(BF16)
    gq = c_g_q.reshape(1, -1)
    gkv = c_g_kv.reshape(1, -1)

    h_p = _mm(xp, w_down, g=g)
    ct_p, st_p = _rope_tables(jnp.arange(seq, dtype=I32))
    q, k, v, lat_p, kr_p = _c_prep(h_p, ct_p, st_p, gq, gkv, wqa, wqb, wuk, wuv)
    vt = jnp.transpose(v.reshape(bsz, seq, -1), (0, 2, 1))
    ot = _c_flash(q.reshape(bsz, seq, -1), k.reshape(bsz, seq, -1), vt)
    xp = _mm(jnp.transpose(ot, (0, 2, 1)).reshape(bsz * seq, -1), w_out, res=xp)

    h_s = _mm(xs, w_down, g=g)
    ct_s, st_s = _rope_tables(jnp.full((nb,), past, I32))
    q_s, _, _, lat_s, kr_s = _c_prep(h_s, ct_s, st_s, gq, gkv, wqa, wqb, wuk, wuv)
    wukt = jnp.concatenate([jnp.transpose(c_w_uk, (1, 2, 0)),
                            jnp.zeros((C_HEADS, C_QW - C_NOPE, C_KV_LORA), F32)], axis=1).astype(BF16)
    ql = _head_mm(q_s, wukt, out_dtype=BF16).reshape(nb, C_HEADS, C_KV_LORA)
    qr = q_s.reshape(nb, C_HEADS, C_QW)[:, :, C_NOPE:C_NOPE + C_ROPE]
    o_lat = _c_decode(page_table.reshape(-1), n_pages, lat_cache, jnp.transpose(rope_cache, (0, 2, 1)),
                      ql, qr, lat_s.reshape(nb, 1, -1), kr_s.reshape(nb, 1, -1))
    wuv2 = jnp.transpose(c_w_uv, (1, 0, 2)).reshape(C_HEADS // 2, 2, C_KV_LORA, C_V)
    z = jnp.zeros_like(wuv2[:, 0])
    wuv_pair = jnp.concatenate([jnp.concatenate([wuv2[:, 0], z], axis=2),
                                jnp.concatenate([z, wuv2[:, 1]], axis=2)], axis=1).astype(BF16)
    o_s = _head_mm(o_lat.reshape(nb, C_HEADS * C_KV_LORA), wuv_pair, out_dtype=BF16)
    xs = _mm(o_s, w_out, res=xs)
    return (xp, xs, lat_p.reshape(bsz, seq, -1), kr_p.reshape(bsz, seq, -1),
            lat_s.reshape(nb, 1, -1), kr_s.reshape(nb, 1, -1))


D_Q0 = 0
D_KV0 = D_HEADS * D_HD
D_QI0 = D_KV0 + 2 * D_KV_HEADS * D_HD
D_KI0 = D_QI0 + D_IDX_HEADS * D_IDX_DIM
D_WI0 = D_KI0 + D_IDX_DIM
D_HW = D_WI0 + LANES - D_IDX_DIM
INT_MIN = -2 ** 31


def _sort_key(score):
    bits = pltpu.bitcast(score, I32)
    return jnp.where(bits >= 0, bits, bits ^ jnp.int32(0x7FFFFFFF))


CNT_UNROLL = 4
D_TQ = 256


def _kth_largest(key_ref, n_groups, k):
    nq = key_ref.shape[1]
    kf = jnp.float32(k)

    def count(pred_fn):
        def body(gi, acc):
            for u in range(CNT_UNROLL):
                off = pl.multiple_of((gi * CNT_UNROLL + u) * LANES, LANES)
                hit = jnp.where(pred_fn(key_ref[pl.ds(off, LANES), :], off), 1.0, 0.0)
                acc = acc + jnp.sum(hit.reshape(LANES // 8, 8, nq), axis=0)
            return acc
        part = lax.fori_loop(0, n_groups, body, jnp.zeros((8, nq), F32))
        return jnp.sum(part, axis=0, keepdims=True)

    nonneg = count(lambda x, off: x >= 0) >= kf
    prefix = jnp.where(nonneg, jnp.int32(0), jnp.int32(INT_MIN))

    def bit_body(i, prefix):
        cand = prefix | lax.shift_left(jnp.int32(1), 30 - i)
        return jnp.where(count(lambda x, off: x >= cand) >= kf, cand, prefix)

    thr = lax.fori_loop(0, 31, bit_body, prefix)
    n_gt = count(lambda x, off: x > thr)
    n_eq = count(lambda x, off: x == thr)
    need = kf - n_gt

    def tie_search(_):
        row = lax.broadcasted_iota(I32, (LANES, nq), 0)

        def body(i, lo):
            cand = lo + lax.shift_right_logical(jnp.int32(1 << 29), i)
            c = count(lambda x, off: (x == thr) & (row + off < cand))
            return jnp.where(c < need, cand, lo)
        return lax.fori_loop(0, 30, body, jnp.zeros((1, nq), I32)) + 1

    any_tie = jnp.max(jnp.where(n_eq > need, 1.0, 0.0)) > 0.0
    bound = lax.cond(any_tie, tie_search, lambda _: jnp.full((1, nq), 2 ** 30, I32), 0)
    return thr, bound


def _d_prompt_body(q_ref, qi_ref, wit_ref, k_ref, vt_ref, ki_ref, bank_ref, ot_ref,
                   key_ref, m_ref, l_ref, acc_ref, s_ref, p_ref, a_ref, *, k_top, n_bank):
    t = Q_TILE
    tq = D_TQ
    nsub = tq // t
    blk = pl.program_id(1)
    n_chunks = (blk + 1) * nsub
    krow = lax.broadcasted_iota(I32, (t, tq), 0)
    qpos = blk * tq + lax.broadcasted_iota(I32, (t, tq), 1)
    qi = qi_ref[0]
    wit = wit_ref[0] * (D_IDX_HEADS ** -0.5) * (D_IDX_DIM ** -0.5)

    def score_chunk(c, carry):
        off = pl.multiple_of(c * t, t)
        ki = ki_ref[0, pl.ds(off, t), :][:, :D_IDX_DIM].astype(BF16)
        dots = _nt(ki, qi)
        sc = jnp.zeros((t, tq), F32)
        for h in range(D_IDX_HEADS):
            sc = sc + wit[h:h + 1, :] * jnp.maximum(dots[:, h * tq:(h + 1) * tq], 0.0)
        key = _sort_key(sc)
        key_ref[pl.ds(off, t), :] = jnp.where(off + krow <= qpos, key, jnp.int32(INT_MIN))
        return carry

    lax.fori_loop(0, n_chunks, score_chunk, 0)
    n_groups = (n_chunks + CNT_UNROLL - 1) // CNT_UNROLL

    def pad_chunk(c, carry):
        key_ref[pl.ds(pl.multiple_of(c * t, t), t), :] = jnp.full((t, tq), INT_MIN, I32)
        return carry

    lax.fori_loop(n_chunks, n_groups * CNT_UNROLL, pad_chunk, 0)
    thr, bound = _kth_largest(key_ref, n_groups, k_top)
    thr = jnp.maximum(thr, jnp.int32(INT_MIN + 1))

    m_ref[...] = jnp.full(m_ref.shape, M_INIT, F32)
    l_ref[...] = jnp.zeros(l_ref.shape, F32)
    acc_ref[...] = jnp.zeros(acc_ref.shape, F32)
    q = (q_ref[0] * (D_HD ** -0.5)).astype(BF16)
    rep = D_HEADS // D_KV_HEADS

    def attend_chunk(c, carry):
        off = pl.multiple_of(c * t, t)
        key = key_ref[pl.ds(off, t), :]
        sel = (key > thr) | ((key == thr) & (krow + off < bound))
        kc = k_ref[0, pl.ds(off, t), :]
        vtc = vt_ref[0, :, pl.ds(off, t)]
        ds_ = [jnp.clip(blk * nsub + u - c, 0, n_bank - 1) for u in range(nsub)]
        for h in range(D_HEADS):
            gk = h // rep
            s_ref[h] = _nt(kc[:, gk * D_HD:(gk + 1) * D_HD], q[:, h * D_HD:(h + 1) * D_HD])
        for h in range(D_HEADS):
            bias = jnp.concatenate([bank_ref[du, h] for du in ds_], axis=1)
            s = jnp.where(sel, s_ref[h] + bias, NEG)
            s_ref[h] = s
            m_old = m_ref[h]
            m_new = jnp.maximum(m_old, jnp.max(s, axis=0, keepdims=True))
            a_ref[h] = jnp.exp(m_old - m_new)
            m_ref[h] = m_new
        for h in range(D_HEADS):
            p = jnp.exp(s_ref[h] - m_ref[h])
            l_ref[h] = a_ref[h] * l_ref[h] + jnp.sum(p, axis=0, keepdims=True)
            p_ref[h] = p.astype(BF16)
        for h in range(D_HEADS):
            gk = h // rep
            acc_ref[h] = a_ref[h] * acc_ref[h] + _nn(vtc[gk * D_HD:(gk + 1) * D_HD, :], p_ref[h])
        return carry

    lax.fori_loop(0, n_chunks, attend_chunk, 0)
    for h in range(D_HEADS):
        ot_ref[0, h * D_HD:(h + 1) * D_HD, :] = (acc_ref[h] / l_ref[h]).astype(ot_ref.dtype)


def _d_prompt(h3, qir, wit, kb, vt, bank, k_top):
    bsz, seq, _ = h3.shape
    t = Q_TILE
    tq = D_TQ
    assert seq % (CNT_UNROLL * t) == 0 and seq % tq == 0
    n_bank = bank.shape[0]
    body = functools.partial(_d_prompt_body, k_top=k_top, n_bank=n_bank)
    qw = D_HEADS * D_HD
    kw = D_KV_HEADS * D_HD
    return pl.pallas_call(
        body,
        grid=(bsz, seq // tq),
        in_specs=[pl.BlockSpec((1, tq, qw), lambda b, i: (b, i, 0)),
                  pl.BlockSpec((1, D_IDX_HEADS * tq, D_IDX_DIM), lambda b, i: (b, i, 0)),
                  pl.BlockSpec((1, D_IDX_HEADS, tq), lambda b, i: (b, 0, i)),
                  pl.BlockSpec((1, seq, kw), lambda b, i: (b, 0, 0)),
                  pl.BlockSpec((1, kw, seq), lambda b, i: (b, 0, 0)),
                  pl.BlockSpec((1, seq, LANES), lambda b, i: (b, 0, D_KI0 // LANES)),
                  pl.BlockSpec(bank.shape, lambda b, i: (0, 0, 0, 0), pipeline_mode=pl.Buffered(1))],
        out_specs=pl.BlockSpec((1, qw, tq), lambda b, i: (b, 0, i)),
        out_shape=jax.ShapeDtypeStruct((bsz, qw, seq), BF16),
        scratch_shapes=[pltpu.VMEM((seq, tq), I32),
                        pltpu.VMEM((D_HEADS, 1, tq), F32), pltpu.VMEM((D_HEADS, 1, tq), F32),
                        pltpu.VMEM((D_HEADS, D_HD, tq), F32),
                        pltpu.VMEM((D_HEADS, t, tq), F32), pltpu.VMEM((D_HEADS, t, tq), BF16),
                        pltpu.VMEM((D_HEADS, 1, tq), F32)],
        compiler_params=_cp("parallel", "arbitrary"),
        name="d_prompt",
    )(h3, qir, wit, kb, vt, h3, bank)


def _d_index_body(pt_ref, kidx_hbm, qi_ref, wi_ref, kin_ref, sc_ref, self_ref, page_buf, sem, *, n_pages):
    slot = _paged_fetch(pt_ref, n_pages, [(kidx_hbm, page_buf)], sem)
    qi = qi_ref[0]
    wi = wi_ref[0] * (D_IDX_HEADS ** -0.5)
    qb = qi.astype(BF16)
    width = PG * PAGE

    def group(gi, carry):
        parts = []
        for g in range(PG):
            dots = _nn(qb, page_buf[slot, gi * PG + g].astype(BF16)) * (D_IDX_DIM ** -0.5)
            parts.append(jnp.sum(wi * jnp.maximum(dots, 0.0), axis=0, keepdims=True))
        sc_ref[0, :, pl.ds(pl.multiple_of(gi * width, width), width)] = jnp.concatenate(parts, axis=1)
        return carry

    lax.fori_loop(0, n_pages // PG, group, 0)
    kin = kin_ref[0].astype(BF16).astype(F32)
    dots = jnp.sum(qb.astype(F32) * kin, axis=1, keepdims=True) * (D_IDX_DIM ** -0.5)
    own = jnp.sum(wi * jnp.maximum(dots, 0.0), axis=0, keepdims=True)
    lane = lax.broadcasted_iota(I32, (1, LANES), 1)
    self_ref[0] = jnp.where(lane == 0, own, -jnp.inf)


def _d_index(page_flat, n_pages, kidx_t, qi3, wi3, kin3):
    nb = qi3.shape[0]
    assert n_pages % PG == 0
    per = lambda shp: pl.BlockSpec((1,) + shp, lambda b, pt: (b, 0, 0))
    gs = pltpu.PrefetchScalarGridSpec(
        num_scalar_prefetch=1,
        grid=(nb,),
        in_specs=[pl.BlockSpec(memory_space=pl.ANY),
                  per((D_IDX_HEADS, D_IDX_DIM)), per((D_IDX_HEADS, 1)), per((1, D_IDX_DIM))],
        out_specs=[per((1, n_pages * PAGE)), per((1, LANES))],
        scratch_shapes=[pltpu.VMEM((2, n_pages, D_IDX_DIM, PAGE), F32),
                        pltpu.SemaphoreType.DMA((1, 2))],
    )
    return pl.pallas_call(
        functools.partial(_d_index_body, n_pages=n_pages),
        grid_spec=gs,
        out_shape=[jax.ShapeDtypeStruct((nb, 1, n_pages * PAGE), F32),
                   jax.ShapeDtypeStruct((nb, 1, LANES), F32)],
        compiler_params=_cp("arbitrary"),
        name="d_index",
    )(page_flat, kidx_t, qi3, wi3, kin3)


def _d_thr_body(sc_ref, thr_ref, bound_ref, key_ref, *, k_top):
    n_chunks = sc_ref.shape[0] // LANES

    def fill(c, carry):
        off = pl.multiple_of(c * LANES, LANES)
        key_ref[pl.ds(off, LANES), :] = _sort_key(sc_ref[pl.ds(off, LANES), :])
        return carry

    lax.fori_loop(0, n_chunks, fill, 0)
    thr, bound = _kth_largest(key_ref, n_chunks // CNT_UNROLL, k_top)
    thr_ref[...] = thr
    bound_ref[...] = bound


def _d_thr(scores_t, k_top):
    nk, nq = scores_t.shape
    return pl.pallas_call(
        functools.partial(_d_thr_body, k_top=k_top),
        grid=(1,),
        in_specs=[pl.BlockSpec((nk, nq), lambda i: (0, 0))],
        out_specs=[pl.BlockSpec((1, nq), lambda i: (0, 0))] * 2,
        out_shape=[jax.ShapeDtypeStruct((1, nq), I32)] * 2,
        scratch_shapes=[pltpu.VMEM((nk, nq), I32)],
        compiler_params=_cp("arbitrary"),
        name="d_thr",
    )(scores_t)


def _d_decode_body(pt_ref, thr_ref, bnd_ref, kv_hbm, sc_ref, self_ref, q_ref, kvn_ref, g_ref, o_ref,
                   kv_buf, sem, *, n_pages):
    slot = _paged_fetch(pt_ref, n_pages, [(kv_hbm, kv_buf)], sem)
    b = pl.program_id(0)
    thr = thr_ref[b]
    bound = bnd_ref[b]
    kw = D_KV_HEADS * D_HD
    rep = D_HEADS // D_KV_HEADS
    hrow = lax.broadcasted_iota(I32, (D_HEADS, kw), 0)
    hcol = lax.broadcasted_iota(I32, (D_HEADS, kw), 1)
    diag = (hcol // D_HD) == (hrow // rep)
    q = q_ref[0]
    qbd = jnp.where(diag, jnp.concatenate([q] * D_KV_HEADS, axis=1), 0.0).astype(BF16)
    gp = PG // 2
    width = gp * PAGE

    def group(gi, carry):
        m_old, l_old, acc = carry
        p0 = gi * gp
        kt = jnp.concatenate([kv_buf[slot, p0 + g, 0] for g in range(gp)], axis=1).astype(BF16)
        vt = jnp.concatenate([kv_buf[slot, p0 + g, 1] for g in range(gp)], axis=1).astype(BF16)
        off = pl.multiple_of(gi * width, width)
        key = _sort_key(sc_ref[0, :, pl.ds(off, width)])
        pos = off + lax.broadcasted_iota(I32, (1, width), 1)
        sel = (key > thr) | ((key == thr) & (pos < bound))
        s = _nn(qbd, kt) * (D_HD ** -0.5) + g_ref[:, pl.ds(off, width)]
        s = jnp.where(sel, s, NEG)
        m_new = jnp.maximum(m_old, jnp.max(s, axis=-1, keepdims=True))
        alpha = jnp.exp(m_old - m_new)
        p = jnp.where(sel, jnp.exp(s - m_new), 0.0)
        return (m_new, alpha * l_old + jnp.sum(p, axis=-1, keepdims=True),
                alpha * acc + _nt(p.astype(BF16), vt))

    init = (jnp.full((D_HEADS, 1), NEG, F32), jnp.zeros((D_HEADS, 1), F32), jnp.zeros((D_HEADS, kw), F32))
    m_o, l_o, acc_o = lax.fori_loop(0, n_pages // gp, group, init)
    past = n_pages * PAGE
    kn = kvn_ref[0, :, :kw].astype(BF16).astype(F32)
    vn = kvn_ref[0, :, kw:].astype(BF16).astype(F32)
    qf = qbd.astype(F32)
    s_self = jnp.sum(qf * kn, axis=1, keepdims=True) * (D_HD ** -0.5) + g_ref[:, past:past + LANES][:, 0:1]
    key_s = _sort_key(self_ref[0][:, 0:1])
    sel_s = (key_s > thr) | ((key_s == thr) & (past < bound))
    s_self = jnp.where(sel_s, s_self, NEG)
    m_n = jnp.maximum(m_o, s_self)
    a = jnp.exp(m_o - m_n)
    p_s = jnp.where(sel_s, jnp.exp(s_self - m_n), 0.0)
    l = a * l_o + p_s
    acc = a * acc_o + p_s.astype(BF16).astype(F32) * vn
    o_full = jnp.where(diag, acc / l, 0.0)
    o_ref[0] = (o_full[:, 0:D_HD] + o_full[:, D_HD:2 * D_HD]
                + o_full[:, 2 * D_HD:3 * D_HD] + o_full[:, 3 * D_HD:]).astype(o_ref.dtype)


def _d_decode(page_flat, thr, bound, n_pages, kv_t, scores3, self3, q3, kvn3, gtab):
    nb = q3.shape[0]
    kw = D_KV_HEADS * D_HD
    assert n_pages % (PG // 2) == 0
    per = lambda shp: pl.BlockSpec((1,) + shp, lambda b, pt, th, bd: (b, 0, 0))
    gs = pltpu.PrefetchScalarGridSpec(
        num_scalar_prefetch=3,
        grid=(nb,),
        in_specs=[pl.BlockSpec(memory_space=pl.ANY),
                  per((1, n_pages * PAGE)), per((1, LANES)), per((D_HEADS, D_HD)), per((1, 2 * kw)),
                  pl.BlockSpec(gtab.shape, lambda b, pt, th, bd: (0, 0))],
        out_specs=per((D_HEADS, D_HD)),
        scratch_shapes=[pltpu.VMEM((2, n_pages, 2, kw, PAGE), F32),
                        pltpu.SemaphoreType.DMA((1, 2))],
    )
    return pl.pallas_call(
        functools.partial(_d_decode_body, n_pages=n_pages),
        grid_spec=gs,
        out_shape=jax.ShapeDtypeStruct((nb, D_HEADS, D_HD), BF16),
        compiler_params=_cp("arbitrary"),
        name="d_decode",
    )(page_flat, thr, bound, kv_t, scores3, self3, q3, kvn3, gtab)


def _d_bias_by_dist(rel_bias, n):
    return rel_bias[:, :D_HEADS][_rel_bucket(jnp.arange(n, dtype=I32))].astype(F32).T


def _mixer_d(xp, xs, bsz, seq, kv_cache, kidx_cache, page_table, rel_bias, g, d_w_in, d_w_out):
    nb = xs.shape[0]
    n_pages = page_table.shape[1]
    past = n_pages * PAGE
    w_in = jnp.pad(d_w_in, ((0, 0), (0, D_HW - d_w_in.shape[1]))).astype(BF16)
    w_out = d_w_out.astype(BF16)
    kw = D_KV_HEADS * D_HD

    h_p = _mm(xp, w_in, g=g)
    h3 = h_p.reshape(bsz, seq, D_HW)
    t = Q_TILE
    n_bank = min(seq // t, -(-(REL_MAX_DIST + t) // t) + 1)
    vec = _d_bias_by_dist(rel_bias, n_bank * t)
    w = jnp.concatenate([vec, jnp.zeros((D_HEADS, t), F32)], axis=1)
    hank = _toeplitz(w, t, n_bank * t)
    bank = jnp.transpose(hank.reshape(D_HEADS, t, n_bank, t), (2, 0, 1, 3))
    wit = jnp.transpose(h3[:, :, D_WI0:D_WI0 + D_IDX_HEADS], (0, 2, 1))
    kb = h3[:, :, D_KV0:D_KV0 + kw].astype(BF16)
    vt = jnp.transpose(h3[:, :, D_KV0 + kw:D_QI0], (0, 2, 1)).astype(BF16)
    qir = jnp.transpose(h3[:, :, D_QI0:D_KI0].reshape(bsz, seq // D_TQ, D_TQ, D_IDX_HEADS, D_IDX_DIM),
                        (0, 1, 3, 2, 4)).reshape(bsz, seq * D_IDX_HEADS, D_IDX_DIM).astype(BF16)
    ot = _d_prompt(h3, qir, wit, kb, vt, bank, min(D_TOPK_MAX, seq // 4))
    xp = _mm(jnp.transpose(ot, (0, 2, 1)).reshape(bsz * seq, -1), w_out, res=xp)
    kv_p = h3[:, :, D_KV0:D_QI0].reshape(bsz, seq, 2, D_KV_HEADS, D_HD)
    kidx_p = h3[:, :, D_KI0:D_WI0]

    h_s = _mm(xs, w_in, g=g)
    page_flat = page_table.reshape(-1)
    qi3 = h_s[:, D_QI0:D_KI0].reshape(nb, D_IDX_HEADS, D_IDX_DIM)
    wi3 = h_s[:, D_WI0:D_WI0 + D_IDX_HEADS].reshape(nb, D_IDX_HEADS, 1)
    kin3 = h_s[:, D_KI0:D_WI0].reshape(nb, 1, D_IDX_DIM)
    sc3, self3 = _d_index(page_flat, n_pages, jnp.transpose(kidx_cache, (0, 2, 1)), qi3, wi3, kin3)
    n_rows = -(-(past + LANES) // (CNT_UNROLL * LANES)) * (CNT_UNROLL * LANES)
    scores_t = jnp.concatenate([sc3.reshape(nb, past), self3.reshape(nb, LANES),
                                jnp.full((nb, n_rows - past - LANES), -jnp.inf, F32)], axis=1).T
    thr, bound = _d_thr(scores_t, min(D_TOPK_MAX, (past + 1) // 4))
    vec_s = _d_bias_by_dist(rel_bias, past + 1)
    gtab_s = jnp.concatenate([vec_s[:, :0:-1], jnp.broadcast_to(vec_s[:, 0:1], (D_HEADS, LANES))], axis=1)
    kv_t = jnp.transpose(kv_cache, (0, 2, 3, 4, 1)).reshape(kv_cache.shape[0], 2, kw, PAGE)
    o_s = _d_decode(page_flat, thr[0], bound[0], n_pages, kv_t, sc3, self3,
                    h_s[:, :D_KV0].reshape(nb, D_HEADS, D_HD), h_s[:, D_KV0:D_QI0].reshape(nb, 1, 2 * kw), gtab_s)
    xs = _mm(o_s.reshape(nb, -1), w_out, res=xs)
    kv_s = h_s[:, D_KV0:D_QI0].reshape(nb, 1, 2, D_KV_HEADS, D_HD)
    kidx_s = h_s[:, D_KI0:D_WI0].reshape(nb, 1, D_IDX_DIM)
    return xp, xs, kv_p, kidx_p, kv_s, kidx_s


def kernel(x_prompt, x_sample, mem_prompt, cache_a1_kv, cache_a2_kv, cache_a3_kv, state_b_conv,
           cache_c_latent, cache_c_krope, cache_d_kv, cache_d_kidx, cache_mem_kv, page_table,
           rel_bias, g_mix, g_cross, g_ffn, g_final, w_xq, w_xkv, w_xo, w_ffn_in, w_ffn_out,
           a_w_in, a_w_out, b_w_pw1, b_b_pw1, b_w_dw, b_b_dw, b_ln_g, b_ln_b, b_w_pw2, b_b_pw2,
           c_w_down, c_g_q, c_g_kv, c_w_uq, c_w_uk, c_w_uv, c_w_out, d_w_in, d_w_out):
    bsz, seq, d = x_prompt.shape
    nb = x_sample.shape[0]
    assert x_sample.shape[1] == 1
    depth = g_mix.shape[0]
    xp = x_prompt.reshape(bsz * seq, d)
    xs = x_sample.reshape(nb, d)
    mem2 = mem_prompt.reshape(bsz * MEM_LEN, d)
    hw = X_HEADS * X_HD
    mem5 = cache_mem_kv.reshape(depth, nb, MEM_LEN, 2 * X_HEADS, X_HD)
    mem_kv_out = []
    outs = {}
    for i in range(depth):
        kind = i % 4
        if kind == 0:
            xp, xs, a_p, a_s = _mixer_a(xp, xs, bsz, seq, [cache_a1_kv, cache_a2_kv, cache_a3_kv],
                                        rel_bias, g_mix[i], a_w_in, a_w_out)
            outs["a_p"], outs["a_s"] = a_p, a_s
        elif kind == 1:
            xp, xs, conv_p, conv_s = _mixer_b(xp, xs, bsz, seq, state_b_conv, g_mix[i], b_w_pw1, b_b_pw1,
                                              b_w_dw, b_b_dw, b_ln_g, b_ln_b, b_w_pw2, b_b_pw2)
            outs["conv"] = (conv_p, conv_s)
        elif kind == 2:
            xp, xs, lat_p, kr_p, lat_s, kr_s = _mixer_c(xp, xs, bsz, seq, cache_c_latent, cache_c_krope,
                                                        page_table, g_mix[i], c_w_down, c_g_q, c_g_kv,
                                                        c_w_uq, c_w_uk, c_w_uv, c_w_out)
            outs["c"] = (lat_p, kr_p, lat_s, kr_s)
        else:
            xp, xs, kv_p, kidx_p, kv_s, kidx_s = _mixer_d(xp, xs, bsz, seq, cache_d_kv, cache_d_kidx,
                                                          page_table, rel_bias, g_mix[i], d_w_in, d_w_out)
            outs["d"] = (kv_p, kidx_p, kv_s, kidx_s)
        mkv = _mm(mem2, w_xkv[i].astype(BF16))
        mem_kv_out.append(mkv.reshape(bsz, MEM_LEN, 2, X_HEADS, X_HD))
        wq = w_xq[i].astype(BF16)
        wo = w_xo[i].astype(BF16)
        xp = _cross(xp.reshape(bsz, seq, d), g_cross[i], wq, wo, mkv.reshape(bsz, MEM_LEN, 2 * hw)).reshape(bsz * seq, d)
        q_s = _mm(xs, wq, g=g_cross[i]).reshape(nb, X_HEADS, X_HD)
        xs = _mm(_cross_s(q_s, mem5, i).reshape(nb, hw), wo, res=xs)
        w_in = w_ffn_in[i].astype(BF16)
        w_out = w_ffn_out[i].astype(BF16)
        xp = _ffn(xp, g_ffn[i], w_in, w_out)
        xs = _ffn(xs, g_ffn[i], w_in, w_out)
    y_p = _rmsnorm(xp, g_final).reshape(bsz, seq, d)
    y_s = _rmsnorm(xs, g_final).reshape(nb, 1, d)
    a_p, a_s = outs["a_p"], outs["a_s"]
    conv_p, conv_s = outs["conv"]
    lat_p, kr_p, lat_s, kr_s = outs["c"]
    kv_p, kidx_p, kv_s, kidx_s = outs["d"]
    return (y_p, y_s, a_p[0], a_p[1], a_p[2], a_s[0], a_s[1], a_s[2], conv_p, conv_s,
            lat_p, kr_p, lat_s, kr_s, kv_p, kidx_p, kv_s, kidx_s, jnp.stack(mem_kv_out))
```
